```python
import math
import jax, jax.numpy as jnp
from jax import lax
import numpy as np

D_MODEL = 2048
BATCH = 2
SEQ = 4096
DEPTH = 4
DEC_BATCH = 128
DEC_SEQ = 4
PAST_LEN = 8192
PAGE_SIZE = 128

N_EVEN = (DEPTH + 1) // 2
N_ODD = DEPTH // 2
HEAD_DIM = 128
N_HEADS = D_MODEL // HEAD_DIM
A_HEADS = N_HEADS // 2
A_KV_HEADS = 2
A_GROUP = A_HEADS // A_KV_HEADS
A_HALF = HEAD_DIM // 2
B_HEADS = N_HEADS // 2
B_Q_RANK = D_MODEL // 4
B_KV_RANK = D_MODEL // 8
B_NOPE = HEAD_DIM
B_ROPE = 64
B_V = HEAD_DIM
C_HEADS = N_HEADS
C_KV_HEADS = 2
C_GROUP = C_HEADS // C_KV_HEADS
FORGET_BIAS = 3.0
D_FF = 4 * D_MODEL
N_BUCKETS = 32
MAX_DISTANCE = 128
ROPE_THETA = 10000.0
Q_BLOCK = 128
EPS = 1e-6

A_Q_COLS = A_HEADS * 2 * A_HALF
A_K_COLS = A_KV_HEADS * 2 * A_HALF
A_V_COLS = A_KV_HEADS * HEAD_DIM
B_KV_COLS = B_KV_RANK + B_ROPE
EVEN_SPLITS = (A_Q_COLS, A_Q_COLS + A_K_COLS, A_Q_COLS + A_K_COLS + A_V_COLS,
               A_Q_COLS + A_K_COLS + A_V_COLS + B_Q_RANK)
EVEN_IN = EVEN_SPLITS[-1] + B_KV_COLS
EVEN_OUT = A_HEADS * HEAD_DIM + B_HEADS * B_V
C_Q_COLS = C_HEADS * HEAD_DIM
C_KV_COLS = C_KV_HEADS * HEAD_DIM
ODD_SPLITS = (C_Q_COLS, C_Q_COLS + C_KV_COLS, C_Q_COLS + 2 * C_KV_COLS)
ODD_IN = ODD_SPLITS[-1] + C_HEADS
ODD_OUT = C_HEADS * HEAD_DIM

kernel_name = 'hybrid_diffattn_mla_fox_decode_step'


def rms_norm(x, g):
    xf = x.astype(jnp.float32)
    y = xf * lax.rsqrt(jnp.mean(xf * xf, axis=-1, keepdims=True) + EPS)
    return (y * g.astype(jnp.float32)).astype(x.dtype)


def rope(x, pos):
    half = x.shape[-1] // 2
    inv = ROPE_THETA ** (-jnp.arange(half, dtype=jnp.float32) / half)
    ang = pos.astype(jnp.float32)[:, None] * inv[None, :]
    shape = (1, ang.shape[0]) + (1,) * (x.ndim - 3) + (half,)
    cos = jnp.cos(ang).reshape(shape)
    sin = jnp.sin(ang).reshape(shape)
    xf = x.astype(jnp.float32)
    x1, x2 = xf[..., :half], xf[..., half:]
    return jnp.concatenate([x1 * cos - x2 * sin, x2 * cos + x1 * sin], axis=-1).astype(x.dtype)


def t5_bucket(rel):
    n = jnp.maximum(rel, 0)
    max_exact = N_BUCKETS // 2
    nf = jnp.maximum(n, 1).astype(jnp.float32)
    large = max_exact + (jnp.log(nf / max_exact) / math.log(MAX_DISTANCE / max_exact)
                         * (N_BUCKETS - max_exact)).astype(jnp.int32)
    return jnp.where(n < max_exact, n, jnp.minimum(large, N_BUCKETS - 1))


def causal_softmax(s, q_pos, k_pos):
    mask = k_pos[None, :] <= q_pos[:, None]
    return jax.nn.softmax(jnp.where(mask, s, -jnp.inf), axis=-1)


def sweep_queries(fn, qs, q_pos):
    t = q_pos.shape[0]
    if t <= Q_BLOCK or t % Q_BLOCK:
        return fn(qs, q_pos)
    nb = t // Q_BLOCK
    blocks = tuple(jnp.moveaxis(a.reshape(a.shape[0], nb, Q_BLOCK, *a.shape[2:]), 1, 0) for a in qs)
    out = lax.map(lambda args: fn(args[0], args[1]), (blocks, q_pos.reshape(nb, Q_BLOCK)))
    out = jnp.moveaxis(out, 0, 1)
    return out.reshape(out.shape[0], t, *out.shape[3:])


def diff_core(q, q_pos, k, v, k_pos, rel_table, lam):
    b, tq = q.shape[0], q.shape[1]
    tk = k.shape[1]
    s = jnp.einsum('bqkgcd,bskcd->bkgcqs', q, k).astype(jnp.float32) * (A_HALF ** -0.5)
    bucket = t5_bucket(q_pos[:, None] - k_pos[None, :])
    bias = jnp.moveaxis(rel_table[bucket], -1, 0).reshape(A_KV_HEADS, A_GROUP, 1, tq, tk)
    p = causal_softmax(s + bias.astype(jnp.float32), q_pos, k_pos)
    attn = p[:, :, :, 0] - lam * p[:, :, :, 1]
    o = jnp.einsum('bkgqs,bskd->bqkgd', attn.astype(v.dtype), v)
    return o.reshape(b, tq, A_HEADS, HEAD_DIM)


def mla_core(q_lat, q_pe, q_pos, c_kv, k_pe, k_pos):
    s = (jnp.einsum('bqhc,bsc->bhqs', q_lat, c_kv)
         + jnp.einsum('bqhr,bsr->bhqs', q_pe, k_pe)).astype(jnp.float32) * ((B_NOPE + B_ROPE) ** -0.5)
    p = causal_softmax(s, q_pos, k_pos)
    return jnp.einsum('bhqs,bsc->bqhc', p.astype(c_kv.dtype), c_kv)


def fox_core(q, cq, q_pos, k, v, ck, k_pos):
    b, tq = q.shape[0], q.shape[1]
    s = jnp.einsum('bqkgd,bskd->bkgqs', q, k).astype(jnp.float32) * (HEAD_DIM ** -0.5)
    decay = jnp.moveaxis(cq, 1, -1)[..., :, None] - jnp.moveaxis(ck, 1, -1)[..., None, :]
    p = causal_softmax(s + decay, q_pos, k_pos)
    o = jnp.einsum('bkgqs,bskd->bqkgd', p.astype(v.dtype), v)
    return o.reshape(b, tq, C_HEADS, HEAD_DIM)


def sq_relu_ffn(h, w_up, w_down):
    return jnp.square(jax.nn.relu(h @ w_up)) @ w_down


def even_mixer(hn, q_pos, past, layer, rel_table, w_in, lam_vecs, subln_g,
               q_norm_g, w_uq, kv_norm_g, w_uk, w_uv, w_o):
    b, t, _ = hn.shape
    proj = hn @ w_in
    aq, ak, av, bq, bkv = jnp.split(proj, EVEN_SPLITS, axis=-1)
    q_a = aq.reshape(b, t, A_KV_HEADS, A_GROUP, 2, A_HALF)
    k_a = ak.reshape(b, t, A_KV_HEADS, 2 * A_HALF)
    v_a = av.reshape(b, t, A_KV_HEADS, HEAD_DIM)
    c_q = rms_norm(bq, q_norm_g)
    q_b = jnp.einsum('btc,chd->bthd', c_q, w_uq)
    q_pe = rope(q_b[..., B_NOPE:], q_pos)
    q_lat = jnp.einsum('bthd,chd->bthc', q_b[..., :B_NOPE], w_uk)
    c_kv = rms_norm(bkv[..., :B_KV_RANK], kv_norm_g)
    k_pe = rope(bkv[..., B_KV_RANK:], q_pos)
    new_rows = (k_a, v_a, c_kv, k_pe)
    if past is None:
        ka_all, va_all, c_all, kpe_all = new_rows
    else:
        ka_all, va_all, c_all, kpe_all = tuple(jnp.concatenate([p_, n_], axis=1) for p_, n_ in zip(past, new_rows))
    k_pos = jnp.arange(ka_all.shape[1])
    ka_all = ka_all.reshape(*ka_all.shape[:3], 2, A_HALF)
    lf = lam_vecs.astype(jnp.float32)
    lam_init = 0.8 - 0.6 * math.exp(-0.3 * layer)
    lam = jnp.exp(jnp.sum(lf[0] * lf[1])) - jnp.exp(jnp.sum(lf[2] * lf[3])) + lam_init
    a_out = sweep_queries(lambda qs, qp: diff_core(qs[0], qp, ka_all, va_all, k_pos, rel_table, lam), (q_a,), q_pos)
    a_out = rms_norm(a_out, subln_g) * (1.0 - lam_init)
    b_lat = sweep_queries(lambda qs, qp: mla_core(qs[0], qs[1], qp, c_all, kpe_all, k_pos), (q_lat, q_pe), q_pos)
    b_out = jnp.einsum('bthc,chd->bthd', b_lat, w_uv)
    o = jnp.concatenate([a_out.reshape(b, t, -1), b_out.reshape(b, t, -1)], axis=-1) @ w_o
    return o, new_rows


def odd_mixer(hn, q_pos, past, w_in, b_f, w_o):
    b, t, _ = hn.shape
    proj = hn @ w_in
    q, k, v, f = jnp.split(proj, ODD_SPLITS, axis=-1)
    q = q.reshape(b, t, C_KV_HEADS, C_GROUP, HEAD_DIM)
    k_new = k.reshape(b, t, C_KV_HEADS, HEAD_DIM)
    v_new = v.reshape(b, t, C_KV_HEADS, HEAD_DIM)
    logf_new = jax.nn.log_sigmoid((f + b_f).astype(jnp.float32)).astype(hn.dtype)
    new_rows = (k_new, v_new, logf_new)
    if past is None:
        k_all, v_all, logf_all = new_rows
    else:
        k_all, v_all, logf_all = tuple(jnp.concatenate([p_, n_], axis=1) for p_, n_ in zip(past, new_rows))
    tk = k_all.shape[1]
    k_pos = jnp.arange(tk)
    c_all = jnp.cumsum(logf_all.astype(jnp.float32), axis=1)
    ck = c_all.reshape(b, tk, C_KV_HEADS, C_GROUP)
    cq = ck[:, tk - t:]
    o = sweep_queries(lambda qs, qp: fox_core(qs[0], qs[1], qp, k_all, v_all, ck, k_pos), (q, cq), q_pos)
    return o.reshape(b, t, -1) @ w_o, new_rows


def setup_inputs(seed: int = 0) -> dict:
    key = jax.random.key(seed)
    ks = iter(jax.random.split(key, 40))
    def nrm(shape, scale=1.0):
        return jax.random.normal(next(ks), shape, jnp.float32) * scale
    def gain(shape):
        return 1.0 + nrm(shape, 0.05)
    n_pages = PAST_LEN // PAGE_SIZE
    n_used = DEC_BATCH * n_pages
    n_pool = n_used + n_used // 4
    pool = (n_pool, PAGE_SIZE)
    x_prompt = nrm((BATCH, SEQ, D_MODEL))
    x_sample = nrm((DEC_BATCH, DEC_SEQ, D_MODEL))
    cache_a_k = nrm((N_EVEN,) + pool + (A_KV_HEADS, 2 * A_HALF))
    cache_a_v = nrm((N_EVEN,) + pool + (A_KV_HEADS, HEAD_DIM))
    cache_b_latent = nrm((N_EVEN,) + pool + (B_KV_RANK,))
    cache_b_rope = nrm((N_EVEN,) + pool + (B_ROPE,))
    cache_c_k = nrm((N_ODD,) + pool + (C_KV_HEADS, HEAD_DIM))
    cache_c_v = nrm((N_ODD,) + pool + (C_KV_HEADS, HEAD_DIM))
    cache_c_logf = jax.nn.log_sigmoid(FORGET_BIAS + nrm((N_ODD,) + pool + (C_HEADS,)))
    perm = jax.random.permutation(next(ks), n_pool)[:n_used]
    page_table = perm.reshape(DEC_BATCH, n_pages).astype(jnp.int32)
    return {
        'x_prompt': x_prompt,
        'x_sample': x_sample,
        'cache_a_k': cache_a_k,
        'cache_a_v': cache_a_v,
        'cache_b_latent': cache_b_latent,
        'cache_b_rope': cache_b_rope,
        'cache_c_k': cache_c_k,
        'cache_c_v': cache_c_v,
        'cache_c_logf': cache_c_logf,
        'page_table': page_table,
        'rel_bias': nrm((N_BUCKETS, A_HEADS), 0.5),
        'w_in_even': nrm((N_EVEN, D_MODEL, EVEN_IN), D_MODEL ** -0.5),
        'a_lambda': nrm((N_EVEN, 4, A_HALF), 0.1),
        'a_subln_g': gain((N_EVEN, HEAD_DIM)),
        'b_q_norm_g': gain((N_EVEN, B_Q_RANK)),
        'b_w_uq': nrm((N_EVEN, B_Q_RANK, B_HEADS, B_NOPE + B_ROPE), B_Q_RANK ** -0.5),
        'b_kv_norm_g': gain((N_EVEN, B_KV_RANK)),
        'b_w_uk': nrm((N_EVEN, B_KV_RANK, B_HEADS, B_NOPE), B_KV_RANK ** -0.5),
        'b_w_uv': nrm((N_EVEN, B_KV_RANK, B_HEADS, B_V), B_KV_RANK ** -0.5),
        'w_o_even': nrm((N_EVEN, EVEN_OUT, D_MODEL), EVEN_OUT ** -0.5),
        'w_in_odd': nrm((N_ODD, D_MODEL, ODD_IN), D_MODEL ** -0.5),
        'c_forget_b': FORGET_BIAS + nrm((N_ODD, C_HEADS), 0.5),
        'w_o_odd': nrm((N_ODD, ODD_OUT, D_MODEL), ODD_OUT ** -0.5),
        'g_mix_pre': gain((DEPTH, D_MODEL)),
        'g_mix_post': gain((DEPTH, D_MODEL)),
        'g_ffn_pre': gain((DEPTH, D_MODEL)),
        'g_ffn_post': gain((DEPTH, D_MODEL)),
        'w_ffn_up': nrm((DEPTH, D_MODEL, D_FF), D_MODEL ** -0.5),
        'w_ffn_down': nrm((DEPTH, D_FF, D_MODEL), D_FF ** -0.5),
    }


def reference(x_prompt, x_sample, cache_a_k, cache_a_v, cache_b_latent, cache_b_rope,
              cache_c_k, cache_c_v, cache_c_logf, page_table, rel_bias,
              w_in_even, a_lambda, a_subln_g, b_q_norm_g, b_w_uq, b_kv_norm_g, b_w_uk, b_w_uv, w_o_even,
              w_in_odd, c_forget_b, w_o_odd,
              g_mix_pre, g_mix_post, g_ffn_pre, g_ffn_post, w_ffn_up, w_ffn_down):

    def gather(cache, li):
        g = cache[li, page_table]
        return g.reshape(g.shape[0], g.shape[1] * g.shape[2], *g.shape[3:])

    def trunk(x, sample):
        t = x.shape[1]
        q_pos = (PAST_LEN if sample else 0) + jnp.arange(t)
        rows_even, rows_odd = [], []
        for layer in range(DEPTH):
            li = layer // 2
            hn = rms_norm(x, g_mix_pre[layer])
            if layer % 2 == 0:
                past = tuple(gather(c, li) for c in (cache_a_k, cache_a_v, cache_b_latent, cache_b_rope)) if sample else None
                o, rows = even_mixer(hn, q_pos, past, layer, rel_bias, w_in_even[li], a_lambda[li], a_subln_g[li],
                                     b_q_norm_g[li], b_w_uq[li], b_kv_norm_g[li], b_w_uk[li], b_w_uv[li], w_o_even[li])
                rows_even.append(rows)
            else:
                past = tuple(gather(c, li) for c in (cache_c_k, cache_c_v, cache_c_logf)) if sample else None
                o, rows = odd_mixer(hn, q_pos, past, w_in_odd[li], c_forget_b[li], w_o_odd[li])
                rows_odd.append(rows)
            x = x + rms_norm(o, g_mix_post[layer])
            f = sq_relu_ffn(rms_norm(x, g_ffn_pre[layer]), w_ffn_up[layer], w_ffn_down[layer])
            x = x + rms_norm(f, g_ffn_post[layer])
        even = [jnp.stack(r) for r in zip(*rows_even)]
        odd = [jnp.stack(r) for r in zip(*rows_odd)]
        return x, even, odd

    y_prompt, (pa_k, pa_v, pb_lat, pb_rope), (pc_k, pc_v, pc_logf) = trunk(x_prompt, False)
    y_sample, (sa_k, sa_v, sb_lat, sb_rope), (sc_k, sc_v, sc_logf) = trunk(x_sample, True)
    return (y_prompt, y_sample, pa_k, sa_k, pa_v, sa_v, pb_lat, sb_lat, pb_rope, sb_rope,
            pc_k, sc_k, pc_v, sc_v, pc_logf, sc_logf)
```

```python
import functools
import math

import jax
import jax.numpy as jnp
from jax import lax
from jax.experimental import pallas as pl
from jax.experimental.pallas import tpu as pltpu

F32 = jnp.float32
BF16 = jnp.bfloat16

HEAD_DIM = 128
A_KV_HEADS = 2
C_KV_HEADS = 2
B_ROPE = 64
N_BUCKETS = 32
MAX_DISTANCE = 128
ROPE_THETA = 10000.0
EPS = 1e-6
NEG = -1e30
LANE = 128
VMEM_LIMIT = 56 * 1024 * 1024

ATT_BLOCK = 256
PAGES_PER_STEP = 8


def _params(*sem):
    return pltpu.CompilerParams(dimension_semantics=sem, vmem_limit_bytes=VMEM_LIMIT)


def _nt_dot(a, b):
    return lax.dot_general(a, b, (((1,), (1,)), ((), ())), preferred_element_type=F32)


def _rms(x, g):
    var = jnp.mean(x * x, axis=-1, keepdims=True)
    return x * lax.rsqrt(var + EPS) * g


def _norm_matmul_kernel(x_ref, g_ref, w_ref, o_ref, hn_ref, *, act):
    @pl.when(pl.program_id(1) == 0)
    def _():
        hn_ref[...] = _rms(x_ref[...], g_ref[...]).astype(BF16)

    y = jnp.dot(hn_ref[...], w_ref[...], preferred_element_type=F32)
    if act:
        y = jnp.square(jnp.maximum(y, 0.0))
    o_ref[...] = y.astype(o_ref.dtype)


def norm_matmul(x, g, w, *, act=False, out_dtype=F32, tn=512):
    m, d = x.shape
    n = w.shape[1]
    tm = min(m, 512)
    tn = min(n, tn)
    return pl.pallas_call(
        functools.partial(_norm_matmul_kernel, act=act),
        grid=(m // tm, n // tn),
        in_specs=[
            pl.BlockSpec((tm, d), lambda i, j: (i, 0)),
            pl.BlockSpec((1, d), lambda i, j: (0, 0)),
            pl.BlockSpec((d, tn), lambda i, j: (0, j)),
        ],
        out_specs=pl.BlockSpec((tm, tn), lambda i, j: (i, j)),
        out_shape=jax.ShapeDtypeStruct((m, n), out_dtype),
        scratch_shapes=[pltpu.VMEM((tm, d), BF16)],
        compiler_params=_params("parallel", "arbitrary"),
        name="norm_matmul",
    )(x, g.reshape(1, d), w)


def _matmul_norm_res_kernel(h_ref, w_ref, g_ref, x_ref, o_ref, acc_ref):
    k = pl.program_id(1)

    @pl.when(k == 0)
    def _():
        acc_ref[...] = jnp.zeros_like(acc_ref)

    acc_ref[...] += jnp.dot(h_ref[...], w_ref[...], preferred_element_type=F32)

    @pl.when(k == pl.num_programs(1) - 1)
    def _():
        o_ref[...] = x_ref[...] + _rms(acc_ref[...], g_ref[...])


def matmul_norm_residual(h, w, g, x):
    m, kd = h.shape
    d = w.shape[1]
    tm = min(m, 512)
    tk = min(kd, 1024)
    return pl.pallas_call(
        _matmul_norm_res_kernel,
        grid=(m // tm, kd // tk),
        in_specs=[
            pl.BlockSpec((tm, tk), lambda i, k: (i, k)),
            pl.BlockSpec((tk, d), lambda i, k: (k, 0)),
            pl.BlockSpec((1, d), lambda i, k: (0, 0)),
            pl.BlockSpec((tm, d), lambda i, k: (i, 0)),
        ],
        out_specs=pl.BlockSpec((tm, d), lambda i, k: (i, 0)),
        out_shape=jax.ShapeDtypeStruct((m, d), F32),
        scratch_shapes=[pltpu.VMEM((tm, d), F32)],
        compiler_params=_params("parallel", "arbitrary"),
        name="matmul_norm_residual",
    )(h, w, g.reshape(1, d), x)


def _head_matmul_kernel(x_ref, w_ref, o_ref):
    o_ref[...] = jnp.dot(x_ref[...].astype(BF16), w_ref[...],
                         preferred_element_type=F32).astype(o_ref.dtype)


def head_matmul(x, w, *, out_dtype=BF16):
    m = x.shape[0]
    nh_heads, kh, nh = w.shape
    tm = min(m, 512)
    return pl.pallas_call(
        _head_matmul_kernel,
        grid=(m // tm, nh_heads),
        in_specs=[
            pl.BlockSpec((tm, kh), lambda i, h: (i, h)),
            pl.BlockSpec((None, kh, nh), lambda i, h: (h, 0, 0)),
        ],
        out_specs=pl.BlockSpec((tm, nh), lambda i, h: (i, h)),
        out_shape=jax.ShapeDtypeStruct((m, nh_heads * nh), out_dtype),
        compiler_params=_params("parallel", "arbitrary"),
        name="head_matmul",
    )(x, w)


def _rope128(x, cos, sin):
    lane = lax.broadcasted_iota(jnp.int32, x.shape, 1)
    low = (lane % B_ROPE) < (B_ROPE // 2)
    rot = jnp.where(low, -pltpu.roll(x, LANE - B_ROPE // 2, 1), pltpu.roll(x, B_ROPE // 2, 1))
    return x * cos + rot * sin


def _mla_prep_kernel(bq_ref, ckv_ref, kpe_ref, gq_ref, gkv_ref, wuq_ref, wuk_ref, cos_ref, sin_ref,
                     qlat_ref, qpe_ref, ckv_out_ref, ckv_bf_ref, kpe_out_ref, kpe_bf_ref, *, n_heads):
    cos = cos_ref[...]
    sin = sin_ref[...]
    c_q = _rms(bq_ref[...], gq_ref[...]).astype(BF16)
    qb = jnp.dot(c_q, wuq_ref[...], preferred_element_type=F32)
    nope = n_heads * HEAD_DIM
    for h in range(n_heads):
        qn = qb[:, h * HEAD_DIM:(h + 1) * HEAD_DIM].astype(BF16)
        qlat_ref[:, h * 256:(h + 1) * 256] = jnp.dot(
            qn, wuk_ref[h], preferred_element_type=F32).astype(BF16)
    for c in range(n_heads * B_ROPE // LANE):
        x = qb[:, nope + c * LANE: nope + (c + 1) * LANE]
        qpe_ref[:, c * LANE:(c + 1) * LANE] = _rope128(x, cos, sin).astype(BF16)
    c_kv = _rms(ckv_ref[...], gkv_ref[...])
    ckv_out_ref[...] = c_kv
    ckv_bf_ref[...] = c_kv.astype(BF16)
    kr = _rope128(kpe_ref[...], cos, sin)
    lane = lax.broadcasted_iota(jnp.int32, kr.shape, 1)
    kr = jnp.where(lane < B_ROPE, kr, 0.0)
    kd = kr + pltpu.roll(kr, B_ROPE, 1)
    kpe_out_ref[...] = kd
    kpe_bf_ref[...] = kd.astype(BF16)


def mla_prep(proj, gq, gkv, wuq, wukT, cos, sin, cols):
    m = proj.shape[0]
    n_heads, _, kv_rank = wukT.shape
    q_rank = wuq.shape[0]
    tm = min(m, 512)
    assert cols["bq"] % q_rank == 0 and cols["ckv"] % kv_rank == 0 and cols["kpe"] % LANE == 0
    outs = (
        jax.ShapeDtypeStruct((m, n_heads * kv_rank), BF16),
        jax.ShapeDtypeStruct((m, n_heads * B_ROPE), BF16),
        jax.ShapeDtypeStruct((m, kv_rank), F32),
        jax.ShapeDtypeStruct((m, kv_rank), BF16),
        jax.ShapeDtypeStruct((m, LANE), F32),
        jax.ShapeDtypeStruct((m, LANE), BF16),
    )
    row = lambda w: pl.BlockSpec((tm, w), lambda i: (i, 0))
    return pl.pallas_call(
        functools.partial(_mla_prep_kernel, n_heads=n_heads),
        grid=(m // tm,),
        in_specs=[
            pl.BlockSpec((tm, q_rank), lambda i: (i, cols["bq"] // q_rank)),
            pl.BlockSpec((tm, kv_rank), lambda i: (i, cols["ckv"] // kv_rank)),
            pl.BlockSpec((tm, LANE), lambda i: (i, cols["kpe"] // LANE)),
            pl.BlockSpec((1, q_rank), lambda i: (0, 0)),
            pl.BlockSpec((1, kv_rank), lambda i: (0, 0)),
            pl.BlockSpec(wuq.shape, lambda i: (0, 0)),
            pl.BlockSpec(wukT.shape, lambda i: (0, 0, 0)),
            row(LANE), row(LANE),
        ],
        out_specs=[row(n_heads * kv_rank), row(n_heads * B_ROPE), row(kv_rank), row(kv_rank),
                   row(LANE), row(LANE)],
        out_shape=outs,
        compiler_params=_params("parallel"),
        name="mla_prep",
    )(proj, proj, proj, gq.reshape(1, -1), gkv.reshape(1, -1), wuq, wukT, cos, sin)


def _lambda_kernel(a_ref, b_ref, c_ref, d_ref, o_ref):
    s1 = jnp.sum(a_ref[...] * b_ref[...], axis=-1, keepdims=True)
    s2 = jnp.sum(c_ref[...] * d_ref[...], axis=-1, keepdims=True)
    o_ref[...] = jnp.broadcast_to(jnp.exp(s1) - jnp.exp(s2), o_ref.shape)


def lambda_terms(a_lambda):
    n = a_lambda.shape[0]
    parts = [a_lambda[:, i, :] for i in range(4)]
    return pl.pallas_call(
        _lambda_kernel,
        out_shape=jax.ShapeDtypeStruct((n, LANE), F32),
        name="lambda_terms",
    )(*parts)


def _bias_lookup_kernel(tab_ref, bucket_ref, o_ref):
    h = pl.program_id(0)
    bucket = bucket_ref[...]
    out = jnp.full(bucket.shape, tab_ref[h, N_BUCKETS - 1], F32)
    for b in range(N_BUCKETS - 1):
        out = jnp.where(bucket == b, tab_ref[h, b], out)
    o_ref[...] = out


def bias_lookup(rel_bias, bucket):
    n_heads = rel_bias.shape[1]
    r, s = bucket.shape
    return pl.pallas_call(
        _bias_lookup_kernel,
        grid=(n_heads,),
        in_specs=[
            pl.BlockSpec(memory_space=pltpu.SMEM),
            pl.BlockSpec((r, s), lambda h: (0, 0)),
        ],
        out_specs=pl.BlockSpec((None, r, s), lambda h: (h, 0, 0)),
        out_shape=jax.ShapeDtypeStruct((n_heads, r, s), F32),
        compiler_params=_params("arbitrary"),
        name="bias_lookup",
    )(rel_bias.T, bucket)


def t5_bucket(rel):
    n = jnp.maximum(rel, 0)
    max_exact = N_BUCKETS // 2
    nf = jnp.maximum(n, 1).astype(F32)
    large = max_exact + (jnp.log(nf / max_exact) / math.log(MAX_DISTANCE / max_exact)
                         * (N_BUCKETS - max_exact)).astype(jnp.int32)
    return jnp.where(n < max_exact, n, jnp.minimum(large, N_BUCKETS - 1))


def _split3(x):
    x1 = x.astype(BF16)
    r1 = x - x1.astype(F32)
    x2 = r1.astype(BF16)
    r2 = r1 - x2.astype(F32)
    return x1, x2, r2.astype(BF16)


def _logf_cumsum_kernel(f_ref, b_ref, logf_ref, c_ref, carry_ref, *, cumulate):
    x = f_ref[...] + b_ref[...]
    logf = jnp.minimum(x, 0.0) - jnp.log1p(jnp.exp(-jnp.abs(x)))
    logf_ref[...] = logf
    if cumulate:
        @pl.when(pl.program_id(1) == 0)
        def _():
            carry_ref[...] = jnp.zeros_like(carry_ref)

        t = logf.shape[0]
        tri = (lax.broadcasted_iota(jnp.int32, (t, t), 0)
               >= lax.broadcasted_iota(jnp.int32, (t, t), 1)).astype(BF16)
        c = carry_ref[...]
        for piece in _split3(logf):
            c = c + jnp.dot(tri, piece, preferred_element_type=F32)
        c_ref[...] = c
        carry_ref[...] = c[t - 1:t, :]
    else:
        c_ref[...] = logf


def logf_cumsum(proj, col, b_f, batch, *, cumulate):
    m = proj.shape[0]
    t = m // batch
    tt = min(t, 256) if cumulate else min(m, 512)
    nb = t // tt if cumulate else m // tt
    grid = (batch, nb) if cumulate else (1, nb)
    bpad = jnp.zeros((1, LANE), F32).at[0, :b_f.shape[0]].set(b_f)
    spec = pl.BlockSpec((tt, LANE), lambda b, i: (b * nb + i, 0))
    return pl.pallas_call(
        functools.partial(_logf_cumsum_kernel, cumulate=cumulate),
        grid=grid,
        in_specs=[
            pl.BlockSpec((tt, LANE), lambda b, i: (b * nb + i, col // LANE)),
            pl.BlockSpec((1, LANE), lambda b, i: (0, 0)),
        ],
        out_specs=[spec, spec],
        out_shape=(jax.ShapeDtypeStruct((m, LANE), F32), jax.ShapeDtypeStruct((m, LANE), F32)),
        scratch_shapes=[pltpu.VMEM((1, LANE), F32)],
        compiler_params=_params("arbitrary", "arbitrary"),
        name="logf_cumsum",
    )(proj, bpad)


def _online_update(state, s, v):
    m, l, acc = state
    m_new = jnp.maximum(m, jnp.max(s, axis=-1, keepdims=True))
    alpha = jnp.exp(m - m_new)
    p = jnp.exp(s - m_new)
    l = alpha * l + jnp.sum(p, axis=-1, keepdims=True)
    acc = alpha * acc + jnp.dot(p.astype(BF16), v, preferred_element_type=F32)
    return m_new, l, acc


def _init_state(rows, width):
    return (jnp.full((rows, 1), NEG, F32), jnp.zeros((rows, 1), F32), jnp.zeros((rows, width), F32))


def _causal_sweep(step, init, qi):
    state = lax.fori_loop(0, qi, lambda j, st: step(j, st, False), init)
    return step(qi, state, True)


def _local_causal(blk):
    return (lax.broadcasted_iota(jnp.int32, (blk, blk), 1)
            <= lax.broadcasted_iota(jnp.int32, (blk, blk), 0))


def _attn_a_kernel(lam_ref, q_ref, k_ref, v_ref, bias_ref, g_ref, o_ref, *, blk, lam_init, n_far):
    qi = pl.program_id(2)
    scale = (HEAD_DIM // 2) ** -0.5
    q = q_ref[...]
    lane = lax.broadcasted_iota(jnp.int32, q.shape, 1)
    qc = (jnp.where(lane < HEAD_DIM // 2, q, 0.0).astype(BF16),
          jnp.where(lane >= HEAD_DIM // 2, q, 0.0).astype(BF16))
    causal = _local_causal(blk)

    def step(j, state, masked):
        start = pl.multiple_of(j * blk, blk)
        k = k_ref[pl.ds(start, blk), :]
        v = v_ref[pl.ds(start, blk), :]
        bias = bias_ref[jnp.minimum(qi - j, n_far)]
        out = []
        for c in range(2):
            s = _nt_dot(qc[c], k) * scale + bias
            if masked:
                s = jnp.where(causal, s, NEG)
            out.append(_online_update(state[c], s, v))
        return tuple(out)

    init = (_init_state(blk, HEAD_DIM), _init_state(blk, HEAD_DIM))
    (m0, l0, a0), (m1, l1, a1) = _causal_sweep(step, init, qi)
    lam = lam_ref[0, 0] + lam_init
    o = a0 / l0 - lam * (a1 / l1)
    o_ref[...] = (_rms(o, g_ref[...]) * (1.0 - lam_init)).astype(o_ref.dtype)


def attn_a_prompt(lam, proj, kv_bf, bias, subln_g, batch, n_heads, lam_init, qcol):
    m = proj.shape[0]
    t = m // batch
    blk = min(ATT_BLOCK, t)
    nq = t // blk
    group = n_heads // A_KV_HEADS
    n_far = bias.shape[1] - 1
    return pl.pallas_call(
        functools.partial(_attn_a_kernel, blk=blk, lam_init=lam_init, n_far=n_far),
        grid=(batch, n_heads, nq),
        in_specs=[
            pl.BlockSpec(memory_space=pltpu.SMEM),
            pl.BlockSpec((blk, HEAD_DIM), lambda b, h, i: (b * nq + i, qcol // HEAD_DIM + h)),
            pl.BlockSpec((t, HEAD_DIM), lambda b, h, i: (b, h // group)),
            pl.BlockSpec((t, HEAD_DIM), lambda b, h, i: (b, A_KV_HEADS + h // group)),
            pl.BlockSpec((None,) + bias.shape[1:], lambda b, h, i: (h, 0, 0, 0)),
            pl.BlockSpec((1, HEAD_DIM), lambda b, h, i: (0, 0)),
        ],
        out_specs=pl.BlockSpec((blk, HEAD_DIM), lambda b, h, i: (b * nq + i, h)),
        out_shape=jax.ShapeDtypeStruct((m, n_heads * HEAD_DIM), BF16),
        compiler_params=_params("parallel", "arbitrary", "arbitrary"),
        name="attn_a_prompt",
    )(lam, proj, kv_bf, kv_bf, bias, subln_g.reshape(1, -1))


def _attn_b_kernel(qlat_ref, qpe_ref, ckv_ref, kpe_ref, wuv_ref, o_ref, *, blk, scale):
    h = pl.program_id(1)
    qi = pl.program_id(2)
    qlat = qlat_ref[...]
    qpe = qpe_ref[...]
    lane = lax.broadcasted_iota(jnp.int32, qpe.shape, 1)
    own = (lane // B_ROPE) == (h % (LANE // B_ROPE))
    qpe = jnp.where(own, qpe, jnp.zeros_like(qpe))
    causal = _local_causal(blk)

    def step(j, state, masked):
        start = pl.multiple_of(j * blk, blk)
        ckv = ckv_ref[pl.ds(start, blk), :]
        kpe = kpe_ref[pl.ds(start, blk), :]
        s = (_nt_dot(qlat, ckv) + _nt_dot(qpe, kpe)) * scale
        if masked:
            s = jnp.where(causal, s, NEG)
        return _online_update(state, s, ckv)

    m, l, acc = _causal_sweep(step, _init_state(blk, qlat.shape[1]), qi)
    lat = (acc / l).astype(BF16)
    o_ref[...] = jnp.dot(lat, wuv_ref[...], preferred_element_type=F32).astype(o_ref.dtype)


def attn_b_prompt(qlat, qpe, ckv_bf, kpe_bf, wuv, batch, scale):
    m = qlat.shape[0]
    n_heads, kv_rank, dv = wuv.shape
    t = m // batch
    blk = min(ATT_BLOCK, t)
    nq = t // blk
    per_lane = LANE // B_ROPE
    return pl.pallas_call(
        functools.partial(_attn_b_kernel, blk=blk, scale=scale),
        grid=(batch, n_heads, nq),
        in_specs=[
            pl.BlockSpec((blk, kv_rank), lambda b, h, i: (b * nq + i, h)),
            pl.BlockSpec((blk, LANE), lambda b, h, i: (b * nq + i, h // per_lane)),
            pl.BlockSpec((t, kv_rank), lambda b, h, i: (b, 0)),
            pl.BlockSpec((t, LANE), lambda b, h, i: (b, 0)),
            pl.BlockSpec((None, kv_rank, dv), lambda b, h, i: (h, 0, 0)),
        ],
        out_specs=pl.BlockSpec((blk, dv), lambda b, h, i: (b * nq + i, h)),
        out_shape=jax.ShapeDtypeStruct((m, n_heads * dv), BF16),
        compiler_params=_params("parallel", "arbitrary", "arbitrary"),
        name="attn_b_prompt",
    )(qlat, qpe, ckv_bf, kpe_bf, wuv)


def _attn_c_kernel(q_ref, k_ref, v_ref, c_ref, o_ref, *, blk):
    h = pl.program_id(1)
    qi = pl.program_id(2)
    scale = HEAD_DIM ** -0.5
    q = q_ref[...].astype(BF16)
    causal = _local_causal(blk)
    qstart = pl.multiple_of(qi * blk, blk)
    cq_row = c_ref[pl.ds(h, 1), pl.ds(qstart, blk)]
    eye = (lax.broadcasted_iota(jnp.int32, (blk, blk), 0)
           == lax.broadcasted_iota(jnp.int32, (blk, blk), 1))
    cq = jnp.sum(jnp.where(eye, cq_row, 0.0), axis=-1, keepdims=True)

    def step(j, state, masked):
        start = pl.multiple_of(j * blk, blk)
        k = k_ref[pl.ds(start, blk), :]
        v = v_ref[pl.ds(start, blk), :]
        ck = c_ref[pl.ds(h, 1), pl.ds(start, blk)]
        s = _nt_dot(q, k) * scale + (cq - ck)
        if masked:
            s = jnp.where(causal, s, NEG)
        return _online_update(state, s, v)

    m, l, acc = _causal_sweep(step, _init_state(blk, HEAD_DIM), qi)
    o_ref[...] = (acc / l).astype(o_ref.dtype)


def attn_c_prompt(proj, kv_bf, c_t, batch, n_heads):
    m = proj.shape[0]
    t = m // batch
    blk = min(ATT_BLOCK, t)
    nq = t // blk
    group = n_heads // C_KV_HEADS
    return pl.pallas_call(
        functools.partial(_attn_c_kernel, blk=blk),
        grid=(batch, n_heads, nq),
        in_specs=[
            pl.BlockSpec((blk, HEAD_DIM), lambda b, h, i: (b * nq + i, h)),
            pl.BlockSpec((t, HEAD_DIM), lambda b, h, i: (b, h // group)),
            pl.BlockSpec((t, HEAD_DIM), lambda b, h, i: (b, C_KV_HEADS + h // group)),
            pl.BlockSpec((None, n_heads, t), lambda b, h, i: (b, 0, 0)),
        ],
        out_specs=pl.BlockSpec((blk, HEAD_DIM), lambda b, h, i: (b * nq + i, h)),
        out_shape=jax.ShapeDtypeStruct((m, n_heads * HEAD_DIM), BF16),
        compiler_params=_params("parallel", "arbitrary", "arbitrary"),
        name="attn_c_prompt",
    )(proj, kv_bf, kv_bf, c_t)


def _scratch_update(m_ref, l_ref, acc_ref, s, v_list):
    m = m_ref[...]
    m_new = jnp.maximum(m, jnp.max(s, axis=-1, keepdims=True))
    alpha = jnp.exp(m - m_new)
    p = jnp.exp(s - m_new)
    l_ref[...] = alpha * l_ref[...] + jnp.sum(p, axis=-1, keepdims=True)
    pb = p.astype(BF16)
    per = s.shape[0] // len(v_list)
    pv = [jnp.dot(pb[i * per:(i + 1) * per], v, preferred_element_type=F32) for i, v in enumerate(v_list)]
    acc_ref[...] = alpha * acc_ref[...] + (pv[0] if len(pv) == 1 else jnp.concatenate(pv, axis=0))
    m_ref[...] = m_new


def _head_rows(ref, kh, page, n_kv):
    return ref[pl.ds(kh, page, stride=n_kv), :]


def _stack(parts):
    return parts[0] if len(parts) == 1 else jnp.concatenate(parts, axis=0)


def _scratch_update_row(m_ref, l_ref, acc_ref, s_col, v_row):
    m = m_ref[...]
    m_new = jnp.maximum(m, s_col)
    alpha = jnp.exp(m - m_new)
    p = jnp.exp(s_col - m_new)
    l_ref[...] = alpha * l_ref[...] + p
    acc_ref[...] = alpha * acc_ref[...] + p * v_row
    m_ref[...] = m_new


def _init_scratch(m_ref, l_ref, acc_ref):
    m_ref[...] = jnp.full(m_ref.shape, NEG, F32)
    l_ref[...] = jnp.zeros(l_ref.shape, F32)
    acc_ref[...] = jnp.zeros(acc_ref.shape, F32)


def _dec_a_kernel(pt_ref, lam_ref, q_ref, kn_ref, vn_ref, bias_ref, g_ref, *rest,
                  pg, page, past, n_new, lam_init):
    k_refs = rest[:pg]
    v_refs = rest[pg:2 * pg]
    o_ref, m_ref, l_ref, acc_ref = rest[2 * pg:]
    g = pl.program_id(1)
    scale = (HEAD_DIM // 2) ** -0.5
    q = q_ref[...]
    n_kv, per, _ = q.shape
    rows = n_kv * per

    @pl.when(g == 0)
    def _():
        _init_scratch(m_ref, l_ref, acc_ref)

    q_bf = q.astype(BF16)
    for i in range(pg):
        start = pl.multiple_of((g * pg + i) * page, page)
        s = _stack([_nt_dot(q_bf[kh], _head_rows(k_refs[i], kh, page, n_kv).astype(BF16))
                    for kh in range(n_kv)])
        s = s * scale + bias_ref[:, pl.ds(start, page)]
        _scratch_update(m_ref, l_ref, acc_ref, s,
                        [_head_rows(v_refs[i], kh, page, n_kv).astype(BF16) for kh in range(n_kv)])

    @pl.when(g == pl.num_programs(1) - 1)
    def _():
        t_row = lax.broadcasted_iota(jnp.int32, (rows, 1), 0) % n_new
        kn = kn_ref[...]
        vn = vn_ref[...]
        for j in range(n_new):
            s = _stack([jnp.sum(q[kh] * kn[j:j + 1, kh * HEAD_DIM:(kh + 1) * HEAD_DIM],
                                axis=-1, keepdims=True) for kh in range(n_kv)]) * scale
            s = s + bias_ref[:, past + j:past + j + 1]
            s = jnp.where(t_row >= j, s, NEG)
            v_rows = _stack([jnp.broadcast_to(vn[j:j + 1, kh * HEAD_DIM:(kh + 1) * HEAD_DIM],
                                              (per, HEAD_DIM)) for kh in range(n_kv)])
            _scratch_update_row(m_ref, l_ref, acc_ref, s, v_rows)
        o = acc_ref[...] / l_ref[...]
        lam = lam_ref[0, 0] + lam_init
        half = per // 2
        for kh in range(n_kv):
            base = kh * per
            d = o[base:base + half] - lam * o[base + half:base + per]
            o_ref[kh] = _rms(d, g_ref[...]) * (1.0 - lam_init)


def attn_a_decode(page_table, lam, q4, proj3, cols, bias, subln_g, cache_k, cache_v, li, lam_init):
    nb, n_pages = page_table.shape
    n_kv, per = q4.shape[1], q4.shape[2]
    page = cache_k.shape[2] // n_kv
    width = n_kv * HEAD_DIM
    rows = n_kv * per
    n_new = proj3.shape[1]
    pg = min(PAGES_PER_STEP, n_pages)
    ng = n_pages // pg

    def page_spec(i):
        return pl.BlockSpec((None, None, page * n_kv, HEAD_DIM),
                            lambda b, g, pt: (li, pt[b * n_pages + g * pg + i], 0, 0))

    in_specs = [
        pl.BlockSpec(memory_space=pltpu.SMEM),
        pl.BlockSpec((None, n_kv, per, HEAD_DIM), lambda b, g, pt: (b, 0, 0, 0)),
        pl.BlockSpec((None, n_new, width), lambda b, g, pt: (b, 0, cols["ak"] // width)),
        pl.BlockSpec((None, n_new, width), lambda b, g, pt: (b, 0, cols["av"] // width)),
        pl.BlockSpec(bias.shape, lambda b, g, pt: (0, 0)),
        pl.BlockSpec((1, HEAD_DIM), lambda b, g, pt: (0, 0)),
    ] + [page_spec(i) for i in range(pg)] * 2
    return pl.pallas_call(
        functools.partial(_dec_a_kernel, pg=pg, page=page, past=n_pages * page, n_new=n_new,
                          lam_init=lam_init),
        grid_spec=pltpu.PrefetchScalarGridSpec(
            num_scalar_prefetch=1,
            grid=(nb, ng),
            in_specs=in_specs,
            out_specs=pl.BlockSpec((None, n_kv, per // 2, HEAD_DIM), lambda b, g, pt: (b, 0, 0, 0)),
            scratch_shapes=[pltpu.VMEM((rows, 1), F32), pltpu.VMEM((rows, 1), F32),
                            pltpu.VMEM((rows, HEAD_DIM), F32)],
        ),
        out_shape=jax.ShapeDtypeStruct((nb, n_kv, per // 2, HEAD_DIM), F32),
        compiler_params=_params("parallel", "arbitrary"),
        name="attn_a_decode",
    )(page_table.reshape(-1), lam, q4, proj3, proj3, bias, subln_g.reshape(1, -1),
      *([cache_k] * pg), *([cache_v] * pg))


def _dec_b_kernel(pt_ref, qlat_ref, qpe_ref, cn_ref, kn_ref, *rest, pg, n_new, scale):
    lat_refs = rest[:pg]
    rope_refs = rest[pg:2 * pg]
    o_ref, m_ref, l_ref, acc_ref = rest[2 * pg:]
    g = pl.program_id(1)
    qlat = qlat_ref[...]
    qpe = qpe_ref[...]
    rows = qlat.shape[0]

    @pl.when(g == 0)
    def _():
        _init_scratch(m_ref, l_ref, acc_ref)

    qlat_bf = qlat.astype(BF16)
    qpe_bf = qpe.astype(BF16)
    for i in range(pg):
        lat = lat_refs[i][...].astype(BF16)
        s = (_nt_dot(qlat_bf, lat) + jnp.dot(qpe_bf, rope_refs[i][...].astype(BF16),
                                             preferred_element_type=F32)) * scale
        _scratch_update(m_ref, l_ref, acc_ref, s, [lat])

    @pl.when(g == pl.num_programs(1) - 1)
    def _():
        t_row = lax.broadcasted_iota(jnp.int32, (rows, 1), 0) % n_new
        cn = cn_ref[...]
        kn = kn_ref[...][:, :B_ROPE]
        for j in range(n_new):
            s = (jnp.sum(qlat * cn[j:j + 1, :], axis=-1, keepdims=True)
                 + jnp.sum(qpe * kn[j:j + 1, :], axis=-1, keepdims=True)) * scale
            s = jnp.where(t_row >= j, s, NEG)
            _scratch_update_row(m_ref, l_ref, acc_ref, s, cn[j:j + 1, :])
        o_ref[...] = acc_ref[...] / l_ref[...]


def attn_b_decode(page_table, qlat3, qpe3, ckv3, kpe3, cache_lat, cache_rope_t, li, scale):
    nb, n_pages = page_table.shape
    page = cache_lat.shape[2]
    kv_rank = cache_lat.shape[3]
    rows = qlat3.shape[1]
    n_new = ckv3.shape[1]
    pg = min(PAGES_PER_STEP, n_pages)
    ng = n_pages // pg

    def page_spec(i, shape):
        return pl.BlockSpec((None, None) + shape,
                            lambda b, g, pt: (li, pt[b * n_pages + g * pg + i], 0, 0))

    in_specs = [
        pl.BlockSpec((None, rows, kv_rank), lambda b, g, pt: (b, 0, 0)),
        pl.BlockSpec((None, rows, B_ROPE), lambda b, g, pt: (b, 0, 0)),
        pl.BlockSpec((None, n_new, kv_rank), lambda b, g, pt: (b, 0, 0)),
        pl.BlockSpec((None, n_new, LANE), lambda b, g, pt: (b, 0, 0)),
    ] + [page_spec(i, (page, kv_rank)) for i in range(pg)] + [page_spec(i, (B_ROPE, page)) for i in range(pg)]
    return pl.pallas_call(
        functools.partial(_dec_b_kernel, pg=pg, n_new=n_new, scale=scale),
        grid_spec=pltpu.PrefetchScalarGridSpec(
            num_scalar_prefetch=1,
            grid=(nb, ng),
            in_specs=in_specs,
            out_specs=pl.BlockSpec((None, rows, kv_rank), lambda b, g, pt: (b, 0, 0)),
            scratch_shapes=[pltpu.VMEM((rows, 1), F32), pltpu.VMEM((rows, 1), F32),
                            pltpu.VMEM((rows, kv_rank), F32)],
        ),
        out_shape=jax.ShapeDtypeStruct((nb, rows, kv_rank), F32),
        compiler_params=_params("parallel", "arbitrary"),
        name="attn_b_decode",
    )(page_table.reshape(-1), qlat3, qpe3, ckv3, kpe3, *([cache_lat] * pg), *([cache_rope_t] * pg))


def _dec_c_kernel(pt_ref, q_ref, kn_ref, vn_ref, fn_ref, *rest, pg, page, n_new, n_heads):
    k_refs = rest[:pg]
    v_refs = rest[pg:2 * pg]
    f_refs = rest[2 * pg:3 * pg]
    o_ref, m_ref, l_ref, acc_ref, later_ref = rest[3 * pg:]
    g = pl.program_id(1)
    scale = HEAD_DIM ** -0.5
    q = q_ref[...]
    n_kv, per, _ = q.shape
    rows = n_kv * per
    expand = (lax.broadcasted_iota(jnp.int32, (rows, n_heads), 0) // n_new
              == lax.broadcasted_iota(jnp.int32, (rows, n_heads), 1))
    t_row = lax.broadcasted_iota(jnp.int32, (rows, 1), 0) % n_new

    def new_logf_col(j):
        return jnp.sum(jnp.where(expand, fn_ref[...][j:j + 1, :n_heads], 0.0), axis=-1, keepdims=True)

    @pl.when(g == 0)
    def _():
        _init_scratch(m_ref, l_ref, acc_ref)
        kn = kn_ref[...]
        vn = vn_ref[...]
        cols = [new_logf_col(j) for j in range(n_new)]
        total = jnp.zeros((rows, 1), F32)
        for j in range(n_new):
            total = total + jnp.where(t_row >= j, cols[j], 0.0)
        later_ref[...] = total
        for j in range(n_new):
            decay = jnp.zeros((rows, 1), F32)
            for i in range(j + 1, n_new):
                decay = decay + jnp.where(t_row >= i, cols[i], 0.0)
            s = _stack([jnp.sum(q[kh] * kn[j:j + 1, kh * HEAD_DIM:(kh + 1) * HEAD_DIM],
                                axis=-1, keepdims=True) for kh in range(n_kv)]) * scale + decay
            s = jnp.where(t_row >= j, s, NEG)
            v_rows = _stack([jnp.broadcast_to(vn[j:j + 1, kh * HEAD_DIM:(kh + 1) * HEAD_DIM],
                                              (per, HEAD_DIM)) for kh in range(n_kv)])
            _scratch_update_row(m_ref, l_ref, acc_ref, s, v_rows)

    q_bf = q.astype(BF16)
    expand_bf = expand.astype(BF16)
    after = (lax.broadcasted_iota(jnp.int32, (page, page), 0)
             > lax.broadcasted_iota(jnp.int32, (page, page), 1)).astype(BF16)
    for i in range(pg):
        later = later_ref[...]
        decay = later
        page_sum = jnp.zeros((rows, 1), F32)
        for piece in _split3(f_refs[i][...]):
            a = jnp.dot(expand_bf, piece, preferred_element_type=F32)
            decay = decay + jnp.dot(a.astype(BF16), after, preferred_element_type=F32)
            page_sum = page_sum + jnp.sum(a, axis=-1, keepdims=True)
        s = _stack([_nt_dot(q_bf[kh], _head_rows(k_refs[i], kh, page, n_kv).astype(BF16))
                    for kh in range(n_kv)])
        s = s * scale + decay
        _scratch_update(m_ref, l_ref, acc_ref, s,
                        [_head_rows(v_refs[i], kh, page, n_kv).astype(BF16) for kh in range(n_kv)])
        later_ref[...] = later + page_sum

    @pl.when(g == pl.num_programs(1) - 1)
    def _():
        o = acc_ref[...] / l_ref[...]
        for kh in range(n_kv):
            o_ref[kh] = o[kh * per:(kh + 1) * per]


def attn_c_decode(page_table, q4, proj3, logf3, cols, cache_k, cache_v, cache_f_t, li):
    nb, n_pages = page_table.shape
    n_kv, per = q4.shape[1], q4.shape[2]
    page = cache_k.shape[2] // n_kv
    width = n_kv * HEAD_DIM
    n_heads = cache_f_t.shape[2]
    rows = n_kv * per
    n_new = proj3.shape[1]
    pg = min(PAGES_PER_STEP, n_pages)
    ng = n_pages // pg

    def page_spec(i, shape):
        return pl.BlockSpec((None, None) + shape,
                            lambda b, g, pt: (li, pt[b * n_pages + n_pages - 1 - (g * pg + i)], 0, 0))

    in_specs = [
        pl.BlockSpec((None, n_kv, per, HEAD_DIM), lambda b, g, pt: (b, 0, 0, 0)),
        pl.BlockSpec((None, n_new, width), lambda b, g, pt: (b, 0, cols["k"] // width)),
        pl.BlockSpec((None, n_new, width), lambda b, g, pt: (b, 0, cols["v"] // width)),
        pl.BlockSpec((None, n_new, LANE), lambda b, g, pt: (b, 0, 0)),
    ] + ([page_spec(i, (page * n_kv, HEAD_DIM)) for i in range(pg)] * 2) \
      + [page_spec(i, (n_heads, page)) for i in range(pg)]
    return pl.pallas_call(
        functools.partial(_dec_c_kernel, pg=pg, page=page, n_new=n_new, n_heads=n_heads),
        grid_spec=pltpu.PrefetchScalarGridSpec(
            num_scalar_prefetch=1,
            grid=(nb, ng),
            in_specs=in_specs,
            out_specs=pl.BlockSpec((None, n_kv, per, HEAD_DIM), lambda b, g, pt: (b, 0, 0, 0)),
            scratch_shapes=[pltpu.VMEM((rows, 1), F32), pltpu.VMEM((rows, 1), F32),
                            pltpu.VMEM((rows, HEAD_DIM), F32), pltpu.VMEM((rows, 1), F32)],
        ),
        out_shape=jax.ShapeDtypeStruct((nb, n_kv, per, HEAD_DIM), F32),
        compiler_params=_params("parallel", "arbitrary"),
        name="attn_c_decode",
    )(page_table.reshape(-1), q4, proj3, proj3, logf3,
      *([cache_k] * pg), *([cache_v] * pg), *([cache_f_t] * pg))


def _pad_cols(w, n):
    return jnp.pad(w, ((0, 0), (0, n - w.shape[1])))


def _rope_tables(pos, reps):
    half = B_ROPE // 2
    inv = ROPE_THETA ** (-jnp.arange(half, dtype=F32) / half)
    ang = pos.astype(F32)[:, None] * inv[None, :]
    cos = jnp.tile(jnp.cos(ang), (reps, LANE // half))
    sin = jnp.tile(jnp.sin(ang), (reps, LANE // half))
    return cos, sin


def _decode_queries(q):
    b, t, n_kv, group, n_maps, w = q.shape
    qp = jnp.transpose(q, (0, 2, 4, 3, 1, 5))
    out = jnp.zeros((b, n_kv, n_maps, group, t, n_maps, w), q.dtype)
    for c in range(n_maps):
        out = out.at[:, :, c, :, :, c, :].set(qp[:, :, c])
    return out.reshape(b, n_kv, n_maps * group * t, n_maps * w)


def kernel(x_prompt, x_sample, cache_a_k, cache_a_v, cache_b_latent, cache_b_rope, cache_c_k, cache_c_v, cache_c_logf, page_table, rel_bias, w_in_even, a_lambda, a_subln_g, b_q_norm_g, b_w_uq, b_kv_norm_g, b_w_uk, b_w_uv, w_o_even, w_in_odd, c_forget_b, w_o_odd, g_mix_pre, g_mix_post, g_ffn_pre, g_ffn_post, w_ffn_up, w_ffn_down):
    batch, seq, d_model = x_prompt.shape
    dec_batch, dec_seq, _ = x_sample.shape
    depth = g_mix_pre.shape[0]
    n_even = w_in_even.shape[0]
    n_pool, page = cache_a_k.shape[1], cache_a_k.shape[2]
    n_pages = page_table.shape[1]
    past = n_pages * page
    a_heads = rel_bias.shape[1]
    a_group = a_heads // A_KV_HEADS
    a_half = HEAD_DIM // 2
    b_heads = b_w_uq.shape[2]
    q_rank = b_w_uq.shape[1]
    kv_rank = b_w_uk.shape[1]
    b_nope = b_w_uk.shape[3]
    c_heads = c_forget_b.shape[1]
    c_group = c_heads // C_KV_HEADS
    b_scale = (b_nope + B_ROPE) ** -0.5

    aq_cols = a_heads * HEAD_DIM
    ak_cols = A_KV_HEADS * HEAD_DIM
    ecols = {"aq": 0, "ak": aq_cols, "av": aq_cols + ak_cols, "bq": aq_cols + 2 * ak_cols}
    ecols["ckv"] = ecols["bq"] + q_rank
    ecols["kpe"] = ecols["ckv"] + kv_rank
    even_in = ecols["kpe"] + B_ROPE
    even_pad = -(-even_in // 512) * 512
    cq_cols = c_heads * HEAD_DIM
    ckv_cols = C_KV_HEADS * HEAD_DIM
    ocols = {"q": 0, "k": cq_cols, "v": cq_cols + ckv_cols, "f": cq_cols + 2 * ckv_cols}
    odd_in = ocols["f"] + c_heads
    odd_pad = -(-(ocols["f"] + LANE) // 512) * 512

    w_in_e = [_pad_cols(w_in_even[i], even_pad).astype(BF16) for i in range(n_even)]
    w_in_o = [_pad_cols(w_in_odd[i], odd_pad).astype(BF16) for i in range(w_in_odd.shape[0])]
    w_uq = [jnp.concatenate([b_w_uq[i][:, :, :b_nope].reshape(q_rank, -1),
                             b_w_uq[i][:, :, b_nope:].reshape(q_rank, -1)], axis=1).astype(BF16)
            for i in range(n_even)]
    w_ukT = [jnp.transpose(b_w_uk[i], (1, 2, 0)).astype(BF16) for i in range(n_even)]
    w_uv = [jnp.transpose(b_w_uv[i], (1, 0, 2)).astype(BF16) for i in range(n_even)]
    w_o_e = w_o_even.astype(BF16)
    w_o_o = w_o_odd.astype(BF16)
    w_up = w_ffn_up.astype(BF16)
    w_down = w_ffn_down.astype(BF16)

    lam_terms = lambda_terms(a_lambda)

    blk = min(ATT_BLOCK, seq)
    n_far = 2
    assert (n_far - 1) * blk + 1 > MAX_DISTANCE or seq // blk <= n_far
    ii = jnp.arange(blk)[:, None]
    jj = jnp.arange(blk)[None, :]
    tiles = jnp.concatenate([t5_bucket(d * blk + ii - jj) for d in range(n_far + 1)], axis=0)
    bias_p = bias_lookup(rel_bias, tiles).reshape(a_heads, n_far + 1, blk, blk)
    q_pos_s = past + jnp.arange(dec_seq)
    k_pos_s = jnp.arange(past + LANE)
    rel_s = jnp.pad(q_pos_s[:, None] - k_pos_s[None, :], ((0, 8 - dec_seq), (0, 0)))
    bias_s = bias_lookup(rel_bias, t5_bucket(rel_s))[:, :dec_seq]
    bias_s = bias_s.reshape(A_KV_HEADS, 1, a_group, dec_seq, -1)
    bias_s = jnp.broadcast_to(bias_s, (A_KV_HEADS, 2, a_group, dec_seq, bias_s.shape[-1]))
    bias_s = bias_s.reshape(A_KV_HEADS * 2 * a_group * dec_seq, -1)

    cos_p, sin_p = _rope_tables(jnp.arange(seq), batch)
    cos_s, sin_s = _rope_tables(q_pos_s, dec_batch)

    ca_k = cache_a_k.reshape(n_even, n_pool, page * A_KV_HEADS, HEAD_DIM)
    ca_v = cache_a_v.reshape(n_even, n_pool, page * A_KV_HEADS, HEAD_DIM)
    cc_k = cache_c_k.reshape(cache_c_k.shape[0], n_pool, page * C_KV_HEADS, HEAD_DIM)
    cc_v = cache_c_v.reshape(cache_c_v.shape[0], n_pool, page * C_KV_HEADS, HEAD_DIM)
    cb_rope_t = jnp.swapaxes(cache_b_rope, 2, 3)
    cc_logf_t = jnp.swapaxes(cache_c_logf, 2, 3)

    def trunk(x, sample):
        nb, t = (dec_batch, dec_seq) if sample else (batch, seq)
        m = nb * t
        x = x.reshape(m, d_model)
        cos, sin = (cos_s, sin_s) if sample else (cos_p, sin_p)
        rows_even, rows_odd = [], []
        for layer in range(depth):
            li = layer // 2
            if layer % 2 == 0:
                lam_init = 0.8 - 0.6 * math.exp(-0.3 * layer)
                lam = lam_terms[li, :1].reshape(1, 1)
                proj = norm_matmul(x, g_mix_pre[layer], w_in_e[li])
                k_a = proj[:, ecols["ak"]:ecols["av"]]
                v_a = proj[:, ecols["av"]:ecols["bq"]]
                qlat, qpe, c_kv, ckv_bf, kpe2, kpe_bf = mla_prep(
                    proj, b_q_norm_g[li], b_kv_norm_g[li], w_uq[li], w_ukT[li], cos, sin, ecols)
                k_pe = kpe2[:, :B_ROPE]
                if sample:
                    q_a = proj[:, :aq_cols].reshape(nb, t, A_KV_HEADS, a_group, 2, a_half)
                    proj3 = proj.reshape(nb, t, -1)
                    oa = attn_a_decode(page_table, lam, _decode_queries(q_a), proj3, ecols, bias_s, a_subln_g[li],
                                       ca_k, ca_v, li, lam_init)
                    a_out = jnp.transpose(oa.reshape(nb, A_KV_HEADS, a_group, t, HEAD_DIM),
                                          (0, 3, 1, 2, 4)).reshape(m, -1).astype(BF16)
                    qlat3 = jnp.transpose(qlat.reshape(nb, t, b_heads, kv_rank).astype(F32),
                                          (0, 2, 1, 3)).reshape(nb, b_heads * t, kv_rank)
                    qpe3 = jnp.transpose(qpe.reshape(nb, t, b_heads, B_ROPE).astype(F32),
                                         (0, 2, 1, 3)).reshape(nb, b_heads * t, B_ROPE)
                    ob = attn_b_decode(page_table, qlat3, qpe3, c_kv.reshape(nb, t, -1),
                                       kpe2.reshape(nb, t, -1), cache_b_latent, cb_rope_t, li, b_scale)
                    lat = jnp.transpose(ob.reshape(nb, b_heads, t, kv_rank), (0, 2, 1, 3)).reshape(m, -1)
                    b_out = head_matmul(lat, w_uv[li])
                else:
                    kv_bf = proj[:, ecols["ak"]:ecols["bq"]].astype(BF16)
                    a_out = attn_a_prompt(lam, proj, kv_bf, bias_p, a_subln_g[li], nb, a_heads,
                                          lam_init, ecols["aq"])
                    b_out = attn_b_prompt(qlat, qpe, ckv_bf, kpe_bf, w_uv[li], nb, b_scale)
                h = jnp.concatenate([a_out, b_out], axis=1)
                x = matmul_norm_residual(h, w_o_e[li], g_mix_post[layer], x)
                rows_even.append((k_a.reshape(nb, t, A_KV_HEADS, HEAD_DIM),
                                  v_a.reshape(nb, t, A_KV_HEADS, HEAD_DIM),
                                  c_kv.reshape(nb, t, kv_rank), k_pe.reshape(nb, t, B_ROPE)))
            else:
                proj = norm_matmul(x, g_mix_pre[layer], w_in_o[li])
                k_c = proj[:, ocols["k"]:ocols["v"]]
                v_c = proj[:, ocols["v"]:ocols["f"]]
                logf_pad, c_pad = logf_cumsum(proj, ocols["f"], c_forget_b[li], nb, cumulate=not sample)
                logf = logf_pad[:, :c_heads]
                if sample:
                    q_c = proj[:, :cq_cols].reshape(nb, t, C_KV_HEADS, c_group, 1, HEAD_DIM)
                    oc = attn_c_decode(page_table, _decode_queries(q_c), proj.reshape(nb, t, -1),
                                       logf_pad.reshape(nb, t, LANE), ocols, cc_k, cc_v, cc_logf_t, li)
                    h = jnp.transpose(oc.reshape(nb, C_KV_HEADS, c_group, t, HEAD_DIM),
                                      (0, 3, 1, 2, 4)).reshape(m, -1).astype(BF16)
                else:
                    kv_bf = proj[:, ocols["k"]:ocols["f"]].astype(BF16)
                    c_t = jnp.transpose(c_pad[:, :c_heads].reshape(nb, t, c_heads), (0, 2, 1))
                    h = attn_c_prompt(proj, kv_bf, c_t, nb, c_heads)
                x = matmul_norm_residual(h, w_o_o[li], g_mix_post[layer], x)
                rows_odd.append((k_c.reshape(nb, t, C_KV_HEADS, HEAD_DIM),
                                 v_c.reshape(nb, t, C_KV_HEADS, HEAD_DIM),
                                 logf.reshape(nb, t, c_heads)))
            u = norm_matmul(x, g_ffn_pre[layer], w_up[layer], act=True, out_dtype=BF16, tn=1024)
            x = matmul_norm_residual(u, w_down[layer], g_ffn_post[layer], x)
        even = [jnp.stack(r) for r in zip(*rows_even)]
        odd = [jnp.stack(r) for r in zip(*rows_odd)]
        return x.reshape(nb, t, d_model), even, odd

    y_prompt, (pa_k, pa_v, pb_lat, pb_rope), (pc_k, pc_v, pc_logf) = trunk(x_prompt, False)
    y_sample, (sa_k, sa_v, sb_lat, sb_rope), (sc_k, sc_v, sc_logf) = trunk(x_sample, True)
    return (y_prompt, y_sample, pa_k, sa_k, pa_v, sa_v, pb_lat, sb_lat, pb_rope, sb_rope,
            pc_k, sc_k, pc_v, sc_v, pc_logf, sc_logf)
```

```python
import functools
import math

import jax
import jax.numpy as jnp
from jax import lax
from jax.experimental import pallas as pl
from jax.experimental.pallas import tpu as pltpu

F32 = jnp.float32
BF16 = jnp.bfloat16

HEAD_DIM = 128
A_KV_HEADS = 2
C_KV_HEADS = 2
B_ROPE = 64
N_BUCKETS = 32
MAX_DISTANCE = 128
ROPE_THETA = 10000.0
EPS = 1e-6
NEG = -1e30
LANE = 128
VMEM_LIMIT = 56 * 1024 * 1024

ATT_BLOCK = 256
PAGES_PER_STEP = 16


def _params(*sem):
    return pltpu.CompilerParams(dimension_semantics=sem, vmem_limit_bytes=VMEM_LIMIT)


def _nt_dot(a, b):
    return lax.dot_general(a, b, (((1,), (1,)), ((), ())), preferred_element_type=F32)


def _rms(x, g):
    var = jnp.mean(x * x, axis=-1, keepdims=True)
    return x * lax.rsqrt(var + EPS) * g


def _norm_matmul_kernel(x_ref, g_ref, w_ref, o_ref, hn_ref, *, act):
    @pl.when(pl.program_id(1) == 0)
    def _():
        hn_ref[...] = _rms(x_ref[...], g_ref[...]).astype(BF16)

    y = jnp.dot(hn_ref[...], w_ref[...], preferred_element_type=F32)
    if act:
        y = jnp.square(jnp.maximum(y, 0.0))
    o_ref[...] = y.astype(o_ref.dtype)


def norm_matmul(x, g, w, *, act=False, out_dtype=F32, tn=512):
    m, d = x.shape
    n = w.shape[1]
    tm = min(m, 512)
    tn = min(n, tn)
    return pl.pallas_call(
        functools.partial(_norm_matmul_kernel, act=act),
        grid=(m // tm, n // tn),
        in_specs=[
            pl.BlockSpec((tm, d), lambda i, j: (i, 0)),
            pl.BlockSpec((1, d), lambda i, j: (0, 0)),
            pl.BlockSpec((d, tn), lambda i, j: (0, j)),
        ],
        out_specs=pl.BlockSpec((tm, tn), lambda i, j: (i, j)),
        out_shape=jax.ShapeDtypeStruct((m, n), out_dtype),
        scratch_shapes=[pltpu.VMEM((tm, d), BF16)],
        compiler_params=_params("parallel", "arbitrary"),
        name="norm_matmul",
    )(x, g.reshape(1, d), w)


def _matmul_norm_res_kernel(h_ref, w_ref, g_ref, x_ref, o_ref, acc_ref):
    k = pl.program_id(1)

    @pl.when(k == 0)
    def _():
        acc_ref[...] = jnp.zeros_like(acc_ref)

    acc_ref[...] += jnp.dot(h_ref[...], w_ref[...], preferred_element_type=F32)

    @pl.when(k == pl.num_programs(1) - 1)
    def _():
        o_ref[...] = x_ref[...] + _rms(acc_ref[...], g_ref[...])


def matmul_norm_residual(h, w, g, x):
    m, kd = h.shape
    d = w.shape[1]
    tm = min(m, 512)
    tk = min(kd, 1024)
    return pl.pallas_call(
        _matmul_norm_res_kernel,
        grid=(m // tm, kd // tk),
        in_specs=[
            pl.BlockSpec((tm, tk), lambda i, k: (i, k)),
            pl.BlockSpec((tk, d), lambda i, k: (k, 0)),
            pl.BlockSpec((1, d), lambda i, k: (0, 0)),
            pl.BlockSpec((tm, d), lambda i, k: (i, 0)),
        ],
        out_specs=pl.BlockSpec((tm, d), lambda i, k: (i, 0)),
        out_shape=jax.ShapeDtypeStruct((m, d), F32),
        scratch_shapes=[pltpu.VMEM((tm, d), F32)],
        compiler_params=_params("parallel", "arbitrary"),
        name="matmul_norm_residual",
    )(h, w, g.reshape(1, d), x)


def _head_matmul_kernel(x_ref, w_ref, o_ref):
    o_ref[...] = jnp.dot(x_ref[...].astype(BF16), w_ref[...],
                         preferred_element_type=F32).astype(o_ref.dtype)


def head_matmul(x, w, *, out_dtype=BF16):
    m = x.shape[0]
    nh_heads, kh, nh = w.shape
    tm = min(m, 512)
    return pl.pallas_call(
        _head_matmul_kernel,
        grid=(m // tm, nh_heads),
        in_specs=[
            pl.BlockSpec((tm, kh), lambda i, h: (i, h)),
            pl.BlockSpec((None, kh, nh), lambda i, h: (h, 0, 0)),
        ],
        out_specs=pl.BlockSpec((tm, nh), lambda i, h: (i, h)),
        out_shape=jax.ShapeDtypeStruct((m, nh_heads * nh), out_dtype),
        compiler_params=_params("parallel", "arbitrary"),
        name="head_matmul",
    )(x, w)


def _rope128(x, cos, sin):
    lane = lax.broadcasted_iota(jnp.int32, x.shape, 1)
    low = (lane % B_ROPE) < (B_ROPE // 2)
    rot = jnp.where(low, -pltpu.roll(x, LANE - B_ROPE // 2, 1), pltpu.roll(x, B_ROPE // 2, 1))
    return x * cos + rot * sin


def _mla_prep_kernel(bq_ref, ckv_ref, kpe_ref, gq_ref, gkv_ref, wuq_ref, wuk_ref, cos_ref, sin_ref,
                     qlat_ref, qpe_ref, ckv_out_ref, ckv_bf_ref, kpe_out_ref, kpe_bf_ref, *, n_heads):
    cos = cos_ref[...]
    sin = sin_ref[...]
    c_q = _rms(bq_ref[...], gq_ref[...]).astype(BF16)
    qb = jnp.dot(c_q, wuq_ref[...], preferred_element_type=F32)
    nope = n_heads * HEAD_DIM
    for h in range(n_heads):
        qn = qb[:, h * HEAD_DIM:(h + 1) * HEAD_DIM].astype(BF16)
        qlat_ref[:, h * 256:(h + 1) * 256] = jnp.dot(
            qn, wuk_ref[h], preferred_element_type=F32).astype(BF16)
    for c in range(n_heads * B_ROPE // LANE):
        x = qb[:, nope + c * LANE: nope + (c + 1) * LANE]
        qpe_ref[:, c * LANE:(c + 1) * LANE] = _rope128(x, cos, sin).astype(BF16)
    c_kv = _rms(ckv_ref[...], gkv_ref[...])
    ckv_out_ref[...] = c_kv
    ckv_bf_ref[...] = c_kv.astype(BF16)
    kr = _rope128(kpe_ref[...], cos, sin)
    lane = lax.broadcasted_iota(jnp.int32, kr.shape, 1)
    kr = jnp.where(lane < B_ROPE, kr, 0.0)
    kd = kr + pltpu.roll(kr, B_ROPE, 1)
    kpe_out_ref[...] = kd
    kpe_bf_ref[...] = kd.astype(BF16)


def mla_prep(proj, gq, gkv, wuq, wukT, cos, sin, cols):
    m = proj.shape[0]
    n_heads, _, kv_rank = wukT.shape
    q_rank = wuq.shape[0]
    tm = min(m, 512)
    assert cols["bq"] % q_rank == 0 and cols["ckv"] % kv_rank == 0 and cols["kpe"] % LANE == 0
    outs = (
        jax.ShapeDtypeStruct((m, n_heads * kv_rank), BF16),
        jax.ShapeDtypeStruct((m, n_heads * B_ROPE), BF16),
        jax.ShapeDtypeStruct((m, kv_rank), F32),
        jax.ShapeDtypeStruct((m, kv_rank), BF16),
        jax.ShapeDtypeStruct((m, LANE), F32),
        jax.ShapeDtypeStruct((m, LANE), BF16),
    )
    row = lambda w: pl.BlockSpec((tm, w), lambda i: (i, 0))
    return pl.pallas_call(
        functools.partial(_mla_prep_kernel, n_heads=n_heads),
        grid=(m // tm,),
        in_specs=[
            pl.BlockSpec((tm, q_rank), lambda i: (i, cols["bq"] // q_rank)),
            pl.BlockSpec((tm, kv_rank), lambda i: (i, cols["ckv"] // kv_rank)),
            pl.BlockSpec((tm, LANE), lambda i: (i, cols["kpe"] // LANE)),
            pl.BlockSpec((1, q_rank), lambda i: (0, 0)),
            pl.BlockSpec((1, kv_rank), lambda i: (0, 0)),
            pl.BlockSpec(wuq.shape, lambda i: (0, 0)),
            pl.BlockSpec(wukT.shape, lambda i: (0, 0, 0)),
            row(LANE), row(LANE),
        ],
        out_specs=[row(n_heads * kv_rank), row(n_heads * B_ROPE), row(kv_rank), row(kv_rank),
                   row(LANE), row(LANE)],
        out_shape=outs,
        compiler_params=_params("parallel"),
        name="mla_prep",
    )(proj, proj, proj, gq.reshape(1, -1), gkv.reshape(1, -1), wuq, wukT, cos, sin)


def _lambda_kernel(a_ref, b_ref, c_ref, d_ref, o_ref):
    s1 = jnp.sum(a_ref[...] * b_ref[...], axis=-1, keepdims=True)
    s2 = jnp.sum(c_ref[...] * d_ref[...], axis=-1, keepdims=True)
    o_ref[...] = jnp.broadcast_to(jnp.exp(s1) - jnp.exp(s2), o_ref.shape)


def lambda_terms(a_lambda):
    n = a_lambda.shape[0]
    parts = [a_lambda[:, i, :] for i in range(4)]
    return pl.pallas_call(
        _lambda_kernel,
        out_shape=jax.ShapeDtypeStruct((n, LANE), F32),
        name="lambda_terms",
    )(*parts)


def _bias_lookup_kernel(tab_ref, bucket_ref, o_ref):
    h = pl.program_id(0)
    bucket = bucket_ref[...]
    out = jnp.full(bucket.shape, tab_ref[h, N_BUCKETS - 1], F32)
    for b in range(N_BUCKETS - 1):
        out = jnp.where(bucket == b, tab_ref[h, b], out)
    o_ref[...] = out


def bias_lookup(rel_bias, bucket):
    n_heads = rel_bias.shape[1]
    r, s = bucket.shape
    return pl.pallas_call(
        _bias_lookup_kernel,
        grid=(n_heads,),
        in_specs=[
            pl.BlockSpec(memory_space=pltpu.SMEM),
            pl.BlockSpec((r, s), lambda h: (0, 0)),
        ],
        out_specs=pl.BlockSpec((None, r, s), lambda h: (h, 0, 0)),
        out_shape=jax.ShapeDtypeStruct((n_heads, r, s), F32),
        compiler_params=_params("arbitrary"),
        name="bias_lookup",
    )(rel_bias.T, bucket)


def t5_bucket(rel):
    n = jnp.maximum(rel, 0)
    max_exact = N_BUCKETS // 2
    nf = jnp.maximum(n, 1).astype(F32)
    large = max_exact + (jnp.log(nf / max_exact) / math.log(MAX_DISTANCE / max_exact)
                         * (N_BUCKETS - max_exact)).astype(jnp.int32)
    return jnp.where(n < max_exact, n, jnp.minimum(large, N_BUCKETS - 1))


def _split3(x):
    x1 = x.astype(BF16)
    r1 = x - x1.astype(F32)
    x2 = r1.astype(BF16)
    r2 = r1 - x2.astype(F32)
    return x1, x2, r2.astype(BF16)


def _logf_cumsum_kernel(f_ref, b_ref, logf_ref, c_ref, carry_ref, *, cumulate):
    x = f_ref[...] + b_ref[...]
    logf = jnp.minimum(x, 0.0) - jnp.log1p(jnp.exp(-jnp.abs(x)))
    logf_ref[...] = logf
    if cumulate:
        @pl.when(pl.program_id(1) == 0)
        def _():
            carry_ref[...] = jnp.zeros_like(carry_ref)

        t = logf.shape[0]
        tri = (lax.broadcasted_iota(jnp.int32, (t, t), 0)
               >= lax.broadcasted_iota(jnp.int32, (t, t), 1)).astype(BF16)
        c = carry_ref[...]
        for piece in _split3(logf):
            c = c + jnp.dot(tri, piece, preferred_element_type=F32)
        c_ref[...] = c
        carry_ref[...] = c[t - 1:t, :]
    else:
        c_ref[...] = logf


def logf_cumsum(proj, col, b_f, batch, *, cumulate):
    m = proj.shape[0]
    t = m // batch
    tt = min(t, 256) if cumulate else min(m, 512)
    nb = t // tt if cumulate else m // tt
    grid = (batch, nb) if cumulate else (1, nb)
    bpad = jnp.zeros((1, LANE), F32).at[0, :b_f.shape[0]].set(b_f)
    spec = pl.BlockSpec((tt, LANE), lambda b, i: (b * nb + i, 0))
    return pl.pallas_call(
        functools.partial(_logf_cumsum_kernel, cumulate=cumulate),
        grid=grid,
        in_specs=[
            pl.BlockSpec((tt, LANE), lambda b, i: (b * nb + i, col // LANE)),
            pl.BlockSpec((1, LANE), lambda b, i: (0, 0)),
        ],
        out_specs=[spec, spec],
        out_shape=(jax.ShapeDtypeStruct((m, LANE), F32), jax.ShapeDtypeStruct((m, LANE), F32)),
        scratch_shapes=[pltpu.VMEM((1, LANE), F32)],
        compiler_params=_params("arbitrary", "arbitrary"),
        name="logf_cumsum",
    )(proj, bpad)


def _online_update(state, s, v):
    m, l, acc = state
    m_new = jnp.maximum(m, jnp.max(s, axis=-1, keepdims=True))
    alpha = jnp.exp(m - m_new)
    p = jnp.exp(s - m_new)
    l = alpha * l + jnp.sum(p, axis=-1, keepdims=True)
    acc = alpha * acc + jnp.dot(p.astype(BF16), v, preferred_element_type=F32)
    return m_new, l, acc


def _init_state(rows, width):
    return (jnp.full((rows, 1), NEG, F32), jnp.zeros((rows, 1), F32), jnp.zeros((rows, width), F32))


def _causal_sweep(step, init, qi):
    state = lax.fori_loop(0, qi, lambda j, st: step(j, st, False), init)
    return step(qi, state, True)


def _local_causal(blk):
    return (lax.broadcasted_iota(jnp.int32, (blk, blk), 1)
            <= lax.broadcasted_iota(jnp.int32, (blk, blk), 0))


def _attn_a_kernel(lam_ref, q_ref, k_ref, v_ref, bias_ref, g_ref, o_ref, *, blk, lam_init, n_far):
    qi = pl.program_id(2)
    scale = (HEAD_DIM // 2) ** -0.5
    q = q_ref[...]
    lane = lax.broadcasted_iota(jnp.int32, q.shape, 1)
    qc = (jnp.where(lane < HEAD_DIM // 2, q, 0.0).astype(BF16),
          jnp.where(lane >= HEAD_DIM // 2, q, 0.0).astype(BF16))
    causal = _local_causal(blk)

    def step(j, state, masked):
        start = pl.multiple_of(j * blk, blk)
        k = k_ref[pl.ds(start, blk), :]
        v = v_ref[pl.ds(start, blk), :]
        bias = bias_ref[jnp.minimum(qi - j, n_far)]
        out = []
        for c in range(2):
            s = _nt_dot(qc[c], k) * scale + bias
            if masked:
                s = jnp.where(causal, s, NEG)
            out.append(_online_update(state[c], s, v))
        return tuple(out)

    init = (_init_state(blk, HEAD_DIM), _init_state(blk, HEAD_DIM))
    (m0, l0, a0), (m1, l1, a1) = _causal_sweep(step, init, qi)
    lam = lam_ref[0, 0] + lam_init
    o = a0 / l0 - lam * (a1 / l1)
    o_ref[...] = (_rms(o, g_ref[...]) * (1.0 - lam_init)).astype(o_ref.dtype)


def attn_a_prompt(lam, proj, kv_bf, bias, subln_g, batch, n_heads, lam_init, qcol):
    m = proj.shape[0]
    t = m // batch
    blk = min(ATT_BLOCK, t)
    nq = t // blk
    group = n_heads // A_KV_HEADS
    n_far = bias.shape[1] - 1
    return pl.pallas_call(
        functools.partial(_attn_a_kernel, blk=blk, lam_init=lam_init, n_far=n_far),
        grid=(batch, n_heads, nq),
        in_specs=[
            pl.BlockSpec(memory_space=pltpu.SMEM),
            pl.BlockSpec((blk, HEAD_DIM), lambda b, h, i: (b * nq + i, qcol // HEAD_DIM + h)),
            pl.BlockSpec((t, HEAD_DIM), lambda b, h, i: (b, h // group)),
            pl.BlockSpec((t, HEAD_DIM), lambda b, h, i: (b, A_KV_HEADS + h // group)),
            pl.BlockSpec((None,) + bias.shape[1:], lambda b, h, i: (h, 0, 0, 0)),
            pl.BlockSpec((1, HEAD_DIM), lambda b, h, i: (0, 0)),
        ],
        out_specs=pl.BlockSpec((blk, HEAD_DIM), lambda b, h, i: (b * nq + i, h)),
        out_shape=jax.ShapeDtypeStruct((m, n_heads * HEAD_DIM), BF16),
        compiler_params=_params("parallel", "arbitrary", "arbitrary"),
        name="attn_a_prompt",
    )(lam, proj, kv_bf, kv_bf, bias, subln_g.reshape(1, -1))


def _attn_b_kernel(qlat_ref, qpe_ref, ckv_ref, kpe_ref, wuv_ref, o_ref, *, blk, scale, hp):
    qi = pl.program_id(2)
    kv_rank = ckv_ref.shape[1]
    dv = wuv_ref.shape[2]
    qpe = qpe_ref[...]
    lane = lax.broadcasted_iota(jnp.int32, qpe.shape, 1)
    qlat = [qlat_ref[:, u * kv_rank:(u + 1) * kv_rank] for u in range(hp)]
    qrope = [jnp.where((lane // B_ROPE) == u, qpe, jnp.zeros_like(qpe)) for u in range(hp)]
    causal = _local_causal(blk)

    def step(j, state, masked):
        start = pl.multiple_of(j * blk, blk)
        ckv = ckv_ref[pl.ds(start, blk), :]
        kpe = kpe_ref[pl.ds(start, blk), :]
        out = []
        for u in range(hp):
            s = (_nt_dot(qlat[u], ckv) + _nt_dot(qrope[u], kpe)) * scale
            if masked:
                s = jnp.where(causal, s, NEG)
            out.append(_online_update(state[u], s, ckv))
        return tuple(out)

    final = _causal_sweep(step, tuple(_init_state(blk, kv_rank) for _ in range(hp)), qi)
    for u, (m, l, acc) in enumerate(final):
        lat = (acc / l).astype(BF16)
        o_ref[:, u * dv:(u + 1) * dv] = jnp.dot(lat, wuv_ref[u],
                                                preferred_element_type=F32).astype(o_ref.dtype)


def attn_b_prompt(qlat, qpe, ckv_bf, kpe_bf, wuv, batch, scale):
    m = qlat.shape[0]
    n_heads, kv_rank, dv = wuv.shape
    t = m // batch
    blk = min(ATT_BLOCK, t)
    nq = t // blk
    hp = LANE // B_ROPE
    return pl.pallas_call(
        functools.partial(_attn_b_kernel, blk=blk, scale=scale, hp=hp),
        grid=(batch, n_heads // hp, nq),
        in_specs=[
            pl.BlockSpec((blk, hp * kv_rank), lambda b, h, i: (b * nq + i, h)),
            pl.BlockSpec((blk, LANE), lambda b, h, i: (b * nq + i, h)),
            pl.BlockSpec((t, kv_rank), lambda b, h, i: (b, 0)),
            pl.BlockSpec((t, LANE), lambda b, h, i: (b, 0)),
            pl.BlockSpec((hp, kv_rank, dv), lambda b, h, i: (h, 0, 0)),
        ],
        out_specs=pl.BlockSpec((blk, hp * dv), lambda b, h, i: (b * nq + i, h)),
        out_shape=jax.ShapeDtypeStruct((m, n_heads * dv), BF16),
        compiler_params=_params("parallel", "arbitrary", "arbitrary"),
        name="attn_b_prompt",
    )(qlat, qpe, ckv_bf, kpe_bf, wuv)


def _attn_c_kernel(q_ref, k_ref, v_ref, c_ref, o_ref, *, blk, hp):
    hq = pl.program_id(1)
    qi = pl.program_id(2)
    scale = HEAD_DIM ** -0.5
    causal = _local_causal(blk)
    qstart = pl.multiple_of(qi * blk, blk)
    eye = (lax.broadcasted_iota(jnp.int32, (blk, blk), 0)
           == lax.broadcasted_iota(jnp.int32, (blk, blk), 1))
    q, cq = [], []
    for u in range(hp):
        q.append(q_ref[:, u * HEAD_DIM:(u + 1) * HEAD_DIM].astype(BF16))
        cq_row = c_ref[pl.ds(hq * hp + u, 1), pl.ds(qstart, blk)]
        cq.append(jnp.sum(jnp.where(eye, cq_row, 0.0), axis=-1, keepdims=True))

    def step(j, state, masked):
        start = pl.multiple_of(j * blk, blk)
        k = k_ref[pl.ds(start, blk), :]
        v = v_ref[pl.ds(start, blk), :]
        out = []
        for u in range(hp):
            ck = c_ref[pl.ds(hq * hp + u, 1), pl.ds(start, blk)]
            s = _nt_dot(q[u], k) * scale + (cq[u] - ck)
            if masked:
                s = jnp.where(causal, s, NEG)
            out.append(_online_update(state[u], s, v))
        return tuple(out)

    final = _causal_sweep(step, tuple(_init_state(blk, HEAD_DIM) for _ in range(hp)), qi)
    for u, (m, l, acc) in enumerate(final):
        o_ref[:, u * HEAD_DIM:(u + 1) * HEAD_DIM] = (acc / l).astype(o_ref.dtype)


def attn_c_prompt(proj, kv_bf, c_t, batch, n_heads, hp=2):
    m = proj.shape[0]
    t = m // batch
    blk = min(ATT_BLOCK, t)
    nq = t // blk
    group = n_heads // C_KV_HEADS
    assert group % hp == 0
    return pl.pallas_call(
        functools.partial(_attn_c_kernel, blk=blk, hp=hp),
        grid=(batch, n_heads // hp, nq),
        in_specs=[
            pl.BlockSpec((blk, hp * HEAD_DIM), lambda b, h, i: (b * nq + i, h)),
            pl.BlockSpec((t, HEAD_DIM), lambda b, h, i: (b, h * hp // group)),
            pl.BlockSpec((t, HEAD_DIM), lambda b, h, i: (b, C_KV_HEADS + h * hp // group)),
            pl.BlockSpec((None, n_heads, t), lambda b, h, i: (b, 0, 0)),
        ],
        out_specs=pl.BlockSpec((blk, hp * HEAD_DIM), lambda b, h, i: (b * nq + i, h)),
        out_shape=jax.ShapeDtypeStruct((m, n_heads * HEAD_DIM), BF16),
        compiler_params=_params("parallel", "arbitrary", "arbitrary"),
        name="attn_c_prompt",
    )(proj, kv_bf, kv_bf, c_t)


def _scratch_update(m_ref, l_ref, acc_ref, s, v_list):
    m = m_ref[...]
    m_new = jnp.maximum(m, jnp.max(s, axis=-1, keepdims=True))
    alpha = jnp.exp(m - m_new)
    p = jnp.exp(s - m_new)
    l_ref[...] = alpha * l_ref[...] + jnp.sum(p, axis=-1, keepdims=True)
    pb = p.astype(BF16)
    per = s.shape[0] // len(v_list)
    pv = [jnp.dot(pb[i * per:(i + 1) * per], v, preferred_element_type=F32) for i, v in enumerate(v_list)]
    acc_ref[...] = alpha * acc_ref[...] + (pv[0] if len(pv) == 1 else jnp.concatenate(pv, axis=0))
    m_ref[...] = m_new


def _head_rows(ref, kh, n_keys, n_kv):
    return ref[pl.ds(kh, n_keys, stride=n_kv), :]


def _rows_dst(buf, i):
    n = buf.shape[0] // PAGES_PER_STEP
    return buf.at[pl.ds(i * n, n)]


def _lanes_dst(buf, i):
    n = buf.shape[1] // PAGES_PER_STEP
    return buf.at[:, pl.ds(i * n, n)]


def _page_copies(pt_ref, streams, sem, seq, grp, slot, *, li, n_pages, reverse):
    out = []
    for i in range(PAGES_PER_STEP):
        logical = grp * PAGES_PER_STEP + i
        if reverse:
            logical = n_pages - 1 - logical
        phys = pt_ref[seq * n_pages + logical]
        for a, (hbm, buf, dst) in enumerate(streams):
            out.append(pltpu.make_async_copy(hbm.at[li, phys], dst(buf.at[slot], i), sem.at[a, slot]))
    return out


def _fetch_pages(pt_ref, streams, sem, **kw):
    b, g = pl.program_id(0), pl.program_id(1)
    nb, ng = pl.num_programs(0), pl.num_programs(1)
    step = b * ng + g
    slot = step % 2
    copies = functools.partial(_page_copies, pt_ref, streams, sem, **kw)

    @pl.when(step == 0)
    def _():
        for c in copies(b, g, slot):
            c.start()

    @pl.when(step + 1 < nb * ng)
    def _():
        nxt = step + 1
        for c in copies(nxt // ng, nxt % ng, 1 - slot):
            c.start()

    for c in copies(b, g, slot):
        c.wait()
    return slot


def _stack(parts):
    return parts[0] if len(parts) == 1 else jnp.concatenate(parts, axis=0)


def _scratch_update_row(m_ref, l_ref, acc_ref, s_col, v_row):
    m = m_ref[...]
    m_new = jnp.maximum(m, s_col)
    alpha = jnp.exp(m - m_new)
    p = jnp.exp(s_col - m_new)
    l_ref[...] = alpha * l_ref[...] + p
    acc_ref[...] = alpha * acc_ref[...] + p * v_row
    m_ref[...] = m_new


def _init_scratch(m_ref, l_ref, acc_ref):
    m_ref[...] = jnp.full(m_ref.shape, NEG, F32)
    l_ref[...] = jnp.zeros(l_ref.shape, F32)
    acc_ref[...] = jnp.zeros(acc_ref.shape, F32)


def _dec_a_kernel(pt_ref, lam_ref, q_ref, kn_ref, vn_ref, bias_ref, g_ref, k_hbm, v_hbm,
                  o_ref, kbuf, vbuf, sem, m_ref, l_ref, acc_ref, *, li, page, n_pages, n_new, lam_init):
    g = pl.program_id(1)
    scale = (HEAD_DIM // 2) ** -0.5
    q = q_ref[...]
    n_kv, per, _ = q.shape
    rows = n_kv * per
    nk = PAGES_PER_STEP * page
    past = n_pages * page
    slot = _fetch_pages(pt_ref, [(k_hbm, kbuf, _rows_dst), (v_hbm, vbuf, _rows_dst)], sem,
                        li=li, n_pages=n_pages, reverse=False)

    @pl.when(g == 0)
    def _():
        _init_scratch(m_ref, l_ref, acc_ref)

    q_bf = q.astype(BF16)
    kb, vb = kbuf.at[slot], vbuf.at[slot]
    start = pl.multiple_of(g * nk, nk)
    s = _stack([_nt_dot(q_bf[kh], _head_rows(kb, kh, nk, n_kv).astype(BF16)) for kh in range(n_kv)])
    s = s * scale + bias_ref[:, pl.ds(start, nk)]
    _scratch_update(m_ref, l_ref, acc_ref, s,
                    [_head_rows(vb, kh, nk, n_kv).astype(BF16) for kh in range(n_kv)])

    @pl.when(g == pl.num_programs(1) - 1)
    def _():
        t_row = lax.broadcasted_iota(jnp.int32, (rows, 1), 0) % n_new
        kn = kn_ref[...]
        vn = vn_ref[...]
        for j in range(n_new):
            s = _stack([jnp.sum(q[kh] * kn[j:j + 1, kh * HEAD_DIM:(kh + 1) * HEAD_DIM],
                                axis=-1, keepdims=True) for kh in range(n_kv)]) * scale
            s = s + bias_ref[:, past + j:past + j + 1]
            s = jnp.where(t_row >= j, s, NEG)
            v_rows = _stack([jnp.broadcast_to(vn[j:j + 1, kh * HEAD_DIM:(kh + 1) * HEAD_DIM],
                                              (per, HEAD_DIM)) for kh in range(n_kv)])
            _scratch_update_row(m_ref, l_ref, acc_ref, s, v_rows)
        o = acc_ref[...] / l_ref[...]
        lam = lam_ref[0, 0] + lam_init
        half = per // 2
        for kh in range(n_kv):
            base = kh * per
            d = o[base:base + half] - lam * o[base + half:base + per]
            o_ref[kh] = _rms(d, g_ref[...]) * (1.0 - lam_init)


def attn_a_decode(page_table, lam, q4, proj3, cols, bias, subln_g, cache_k, cache_v, li, lam_init):
    nb, n_pages = page_table.shape
    n_kv, per = q4.shape[1], q4.shape[2]
    page = cache_k.shape[2] // n_kv
    width = n_kv * HEAD_DIM
    rows = n_kv * per
    n_new = proj3.shape[1]
    pg = PAGES_PER_STEP
    assert n_pages % pg == 0
    ng = n_pages // pg
    page_buf = pltpu.VMEM((2, pg * page * n_kv, HEAD_DIM), F32)

    in_specs = [
        pl.BlockSpec(memory_space=pltpu.SMEM),
        pl.BlockSpec((None, n_kv, per, HEAD_DIM), lambda b, g, pt: (b, 0, 0, 0)),
        pl.BlockSpec((None, n_new, width), lambda b, g, pt: (b, 0, cols["ak"] // width)),
        pl.BlockSpec((None, n_new, width), lambda b, g, pt: (b, 0, cols["av"] // width)),
        pl.BlockSpec(bias.shape, lambda b, g, pt: (0, 0)),
        pl.BlockSpec((1, HEAD_DIM), lambda b, g, pt: (0, 0)),
        pl.BlockSpec(memory_space=pl.ANY),
        pl.BlockSpec(memory_space=pl.ANY),
    ]
    return pl.pallas_call(
        functools.partial(_dec_a_kernel, li=li, page=page, n_pages=n_pages, n_new=n_new,
                          lam_init=lam_init),
        grid_spec=pltpu.PrefetchScalarGridSpec(
            num_scalar_prefetch=1,
            grid=(nb, ng),
            in_specs=in_specs,
            out_specs=pl.BlockSpec((None, n_kv, per // 2, HEAD_DIM), lambda b, g, pt: (b, 0, 0, 0)),
            scratch_shapes=[page_buf, page_buf, pltpu.SemaphoreType.DMA((2, 2)),
                            pltpu.VMEM((rows, 1), F32), pltpu.VMEM((rows, 1), F32),
                            pltpu.VMEM((rows, HEAD_DIM), F32)],
        ),
        out_shape=jax.ShapeDtypeStruct((nb, n_kv, per // 2, HEAD_DIM), F32),
        compiler_params=_params("arbitrary", "arbitrary"),
        name="attn_a_decode",
    )(page_table.reshape(-1), lam, q4, proj3, proj3, bias, subln_g.reshape(1, -1), cache_k, cache_v)


def _dec_b_kernel(pt_ref, qlat_ref, qpe_ref, cn_ref, kn_ref, lat_hbm, rope_hbm,
                  o_ref, latbuf, ropebuf, sem, m_ref, l_ref, acc_ref, *, li, n_pages, n_new, scale):
    g = pl.program_id(1)
    qlat = qlat_ref[...]
    qpe = qpe_ref[...]
    rows = qlat.shape[0]
    slot = _fetch_pages(pt_ref, [(lat_hbm, latbuf, _rows_dst), (rope_hbm, ropebuf, _lanes_dst)], sem,
                        li=li, n_pages=n_pages, reverse=False)

    @pl.when(g == 0)
    def _():
        _init_scratch(m_ref, l_ref, acc_ref)

    lat = latbuf[slot].astype(BF16)
    s = (_nt_dot(qlat.astype(BF16), lat)
         + jnp.dot(qpe.astype(BF16), ropebuf[slot].astype(BF16), preferred_element_type=F32)) * scale
    _scratch_update(m_ref, l_ref, acc_ref, s, [lat])

    @pl.when(g == pl.num_programs(1) - 1)
    def _():
        t_row = lax.broadcasted_iota(jnp.int32, (rows, 1), 0) % n_new
        cn = cn_ref[...]
        kn = kn_ref[...][:, :B_ROPE]
        for j in range(n_new):
            s = (jnp.sum(qlat * cn[j:j + 1, :], axis=-1, keepdims=True)
                 + jnp.sum(qpe * kn[j:j + 1, :], axis=-1, keepdims=True)) * scale
            s = jnp.where(t_row >= j, s, NEG)
            _scratch_update_row(m_ref, l_ref, acc_ref, s, cn[j:j + 1, :])
        o_ref[...] = acc_ref[...] / l_ref[...]


def attn_b_decode(page_table, qlat3, qpe3, ckv3, kpe3, cache_lat, cache_rope_t, li, scale):
    nb, n_pages = page_table.shape
    page = cache_lat.shape[2]
    kv_rank = cache_lat.shape[3]
    rows = qlat3.shape[1]
    n_new = ckv3.shape[1]
    pg = PAGES_PER_STEP
    assert n_pages % pg == 0
    ng = n_pages // pg

    in_specs = [
        pl.BlockSpec((None, rows, kv_rank), lambda b, g, pt: (b, 0, 0)),
        pl.BlockSpec((None, rows, B_ROPE), lambda b, g, pt: (b, 0, 0)),
        pl.BlockSpec((None, n_new, kv_rank), lambda b, g, pt: (b, 0, 0)),
        pl.BlockSpec((None, n_new, LANE), lambda b, g, pt: (b, 0, 0)),
        pl.BlockSpec(memory_space=pl.ANY),
        pl.BlockSpec(memory_space=pl.ANY),
    ]
    return pl.pallas_call(
        functools.partial(_dec_b_kernel, li=li, n_pages=n_pages, n_new=n_new, scale=scale),
        grid_spec=pltpu.PrefetchScalarGridSpec(
            num_scalar_prefetch=1,
            grid=(nb, ng),
            in_specs=in_specs,
            out_specs=pl.BlockSpec((None, rows, kv_rank), lambda b, g, pt: (b, 0, 0)),
            scratch_shapes=[pltpu.VMEM((2, pg * page, kv_rank), F32),
                            pltpu.VMEM((2, B_ROPE, pg * page), F32),
                            pltpu.SemaphoreType.DMA((2, 2)),
                            pltpu.VMEM((rows, 1), F32), pltpu.VMEM((rows, 1), F32),
                            pltpu.VMEM((rows, kv_rank), F32)],
        ),
        out_shape=jax.ShapeDtypeStruct((nb, rows, kv_rank), F32),
        compiler_params=_params("arbitrary", "arbitrary"),
        name="attn_b_decode",
    )(page_table.reshape(-1), qlat3, qpe3, ckv3, kpe3, cache_lat, cache_rope_t)


def _dec_c_kernel(pt_ref, q_ref, kn_ref, vn_ref, fn_ref, k_hbm, v_hbm, f_hbm,
                  o_ref, kbuf, vbuf, fbuf, sem, m_ref, l_ref, acc_ref, later_ref, newdecay_ref,
                  *, li, page, n_pages, n_new, n_heads):
    g = pl.program_id(1)
    scale = HEAD_DIM ** -0.5
    q = q_ref[...]
    n_kv, per, _ = q.shape
    rows = n_kv * per
    group = n_heads // n_kv
    nk = PAGES_PER_STEP * page
    row = lax.broadcasted_iota(jnp.int32, (rows, 1), 0)
    t_row = (row % per) // group
    head_row = (row // per) * group + row % group
    expand = head_row == lax.broadcasted_iota(jnp.int32, (rows, n_heads), 1)
    slot = _fetch_pages(pt_ref, [(k_hbm, kbuf, _rows_dst), (v_hbm, vbuf, _rows_dst),
                                 (f_hbm, fbuf, _rows_dst)], sem,
                        li=li, n_pages=n_pages, reverse=True)

    def new_logf_col(j):
        return jnp.sum(jnp.where(expand, fn_ref[...][j:j + 1, :n_heads], 0.0), axis=-1, keepdims=True)

    @pl.when(g == 0)
    def _():
        _init_scratch(m_ref, l_ref, acc_ref)
        kn = kn_ref[...]
        vn = vn_ref[...]
        cols = [new_logf_col(j) for j in range(n_new)]
        total = jnp.zeros((rows, 1), F32)
        for j in range(n_new):
            total = total + jnp.where(t_row >= j, cols[j], 0.0)
        newdecay_ref[...] = total
        later_ref[...] = jnp.zeros(later_ref.shape, F32)
        for j in range(n_new):
            decay = jnp.zeros((rows, 1), F32)
            for i in range(j + 1, n_new):
                decay = decay + jnp.where(t_row >= i, cols[i], 0.0)
            s = _stack([jnp.sum(q[kh] * kn[j:j + 1, kh * HEAD_DIM:(kh + 1) * HEAD_DIM],
                                axis=-1, keepdims=True) for kh in range(n_kv)]) * scale + decay
            s = jnp.where(t_row >= j, s, NEG)
            v_rows = _stack([jnp.broadcast_to(vn[j:j + 1, kh * HEAD_DIM:(kh + 1) * HEAD_DIM],
                                              (per, HEAD_DIM)) for kh in range(n_kv)])
            _scratch_update_row(m_ref, l_ref, acc_ref, s, v_rows)

    q_bf = q.astype(BF16)
    kb, vb = kbuf.at[slot], vbuf.at[slot]
    f = fbuf[slot]
    after = (lax.broadcasted_iota(jnp.int32, (page, page), 0)
             > lax.broadcasted_iota(jnp.int32, (page, page), 1)).astype(BF16)
    within = jnp.zeros(f.shape, F32)
    for piece in _split3(f):
        within = within + jnp.dot(piece, after, preferred_element_type=F32)
    page_sum = jnp.sum(f, axis=-1, keepdims=True)
    later = later_ref[...]
    per_page = []
    for i in range(PAGES_PER_STEP):
        sl = slice(i * n_heads, (i + 1) * n_heads)
        per_page.append(within[sl] + later)
        later = later + page_sum[sl]
    later_ref[...] = later
    by_head = jnp.concatenate(per_page, axis=1)
    decay = _stack([jnp.concatenate([by_head[kh * group:(kh + 1) * group]] * n_new, axis=0)
                    for kh in range(n_kv)])
    s = _stack([_nt_dot(q_bf[kh], _head_rows(kb, kh, nk, n_kv).astype(BF16)) for kh in range(n_kv)])
    s = s * scale + (decay + newdecay_ref[...])
    _scratch_update(m_ref, l_ref, acc_ref, s,
                    [_head_rows(vb, kh, nk, n_kv).astype(BF16) for kh in range(n_kv)])

    @pl.when(g == pl.num_programs(1) - 1)
    def _():
        o = acc_ref[...] / l_ref[...]
        for kh in range(n_kv):
            o_ref[kh] = o[kh * per:(kh + 1) * per]


def attn_c_decode(page_table, q4, proj3, logf3, cols, cache_k, cache_v, cache_f_t, li):
    nb, n_pages = page_table.shape
    n_kv, per = q4.shape[1], q4.shape[2]
    page = cache_k.shape[2] // n_kv
    width = n_kv * HEAD_DIM
    n_heads = cache_f_t.shape[2]
    rows = n_kv * per
    n_new = proj3.shape[1]
    pg = PAGES_PER_STEP
    assert n_pages % pg == 0
    ng = n_pages // pg
    page_buf = pltpu.VMEM((2, pg * page * n_kv, HEAD_DIM), F32)

    in_specs = [
        pl.BlockSpec((None, n_kv, per, HEAD_DIM), lambda b, g, pt: (b, 0, 0, 0)),
        pl.BlockSpec((None, n_new, width), lambda b, g, pt: (b, 0, cols["k"] // width)),
        pl.BlockSpec((None, n_new, width), lambda b, g, pt: (b, 0, cols["v"] // width)),
        pl.BlockSpec((None, n_new, LANE), lambda b, g, pt: (b, 0, 0)),
        pl.BlockSpec(memory_space=pl.ANY),
        pl.BlockSpec(memory_space=pl.ANY),
        pl.BlockSpec(memory_space=pl.ANY),
    ]
    return pl.pallas_call(
        functools.partial(_dec_c_kernel, li=li, page=page, n_pages=n_pages, n_new=n_new, n_heads=n_heads),
        grid_spec=pltpu.PrefetchScalarGridSpec(
            num_scalar_prefetch=1,
            grid=(nb, ng),
            in_specs=in_specs,
            out_specs=pl.BlockSpec((None, n_kv, per, HEAD_DIM), lambda b, g, pt: (b, 0, 0, 0)),
            scratch_shapes=[page_buf, page_buf, pltpu.VMEM((2, pg * n_heads, page), F32),
                            pltpu.SemaphoreType.DMA((3, 2)),
                            pltpu.VMEM((rows, 1), F32), pltpu.VMEM((rows, 1), F32),
                            pltpu.VMEM((rows, HEAD_DIM), F32),
                            pltpu.VMEM((n_heads, 1), F32), pltpu.VMEM((rows, 1), F32)],
        ),
        out_shape=jax.ShapeDtypeStruct((nb, n_kv, per, HEAD_DIM), F32),
        compiler_params=_params("arbitrary", "arbitrary"),
        name="attn_c_decode",
    )(page_table.reshape(-1), q4, proj3, proj3, logf3, cache_k, cache_v, cache_f_t)


def _pad_cols(w, n):
    return jnp.pad(w, ((0, 0), (0, n - w.shape[1])))


def _rope_tables(pos, reps):
    half = B_ROPE // 2
    inv = ROPE_THETA ** (-jnp.arange(half, dtype=F32) / half)
    ang = pos.astype(F32)[:, None] * inv[None, :]
    cos = jnp.tile(jnp.cos(ang), (reps, LANE // half))
    sin = jnp.tile(jnp.sin(ang), (reps, LANE // half))
    return cos, sin


def _decode_queries(q):
    b, t, n_kv, group, n_maps, w = q.shape
    qp = jnp.transpose(q, (0, 2, 4, 3, 1, 5))
    out = jnp.zeros((b, n_kv, n_maps, group, t, n_maps, w), q.dtype)
    for c in range(n_maps):
        out = out.at[:, :, c, :, :, c, :].set(qp[:, :, c])
    return out.reshape(b, n_kv, n_maps * group * t, n_maps * w)


def kernel(x_prompt, x_sample, cache_a_k, cache_a_v, cache_b_latent, cache_b_rope, cache_c_k, cache_c_v, cache_c_logf, page_table, rel_bias, w_in_even, a_lambda, a_subln_g, b_q_norm_g, b_w_uq, b_kv_norm_g, b_w_uk, b_w_uv, w_o_even, w_in_odd, c_forget_b, w_o_odd, g_mix_pre, g_mix_post, g_ffn_pre, g_ffn_post, w_ffn_up, w_ffn_down):
    batch, seq, d_model = x_prompt.shape
    dec_batch, dec_seq, _ = x_sample.shape
    depth = g_mix_pre.shape[0]
    n_even = w_in_even.shape[0]
    n_pool, page = cache_a_k.shape[1], cache_a_k.shape[2]
    n_pages = page_table.shape[1]
    past = n_pages * page
    a_heads = rel_bias.shape[1]
    a_group = a_heads // A_KV_HEADS
    a_half = HEAD_DIM // 2
    b_heads = b_w_uq.shape[2]
    q_rank = b_w_uq.shape[1]
    kv_rank = b_w_uk.shape[1]
    b_nope = b_w_uk.shape[3]
    c_heads = c_forget_b.shape[1]
    c_group = c_heads // C_KV_HEADS
    b_scale = (b_nope + B_ROPE) ** -0.5

    aq_cols = a_heads * HEAD_DIM
    ak_cols = A_KV_HEADS * HEAD_DIM
    ecols = {"aq": 0, "ak": aq_cols, "av": aq_cols + ak_cols, "bq": aq_cols + 2 * ak_cols}
    ecols["ckv"] = ecols["bq"] + q_rank
    ecols["kpe"] = ecols["ckv"] + kv_rank
    even_in = ecols["kpe"] + B_ROPE
    even_pad = -(-even_in // 512) * 512
    cq_cols = c_heads * HEAD_DIM
    ckv_cols = C_KV_HEADS * HEAD_DIM
    ocols = {"q": 0, "k": cq_cols, "v": cq_cols + ckv_cols, "f": cq_cols + 2 * ckv_cols}
    odd_in = ocols["f"] + c_heads
    odd_pad = -(-(ocols["f"] + LANE) // 512) * 512

    w_in_e = [_pad_cols(w_in_even[i], even_pad).astype(BF16) for i in range(n_even)]
    w_in_o = [_pad_cols(w_in_odd[i], odd_pad).astype(BF16) for i in range(w_in_odd.shape[0])]
    w_uq = [jnp.concatenate([b_w_uq[i][:, :, :b_nope].reshape(q_rank, -1),
                             b_w_uq[i][:, :, b_nope:].reshape(q_rank, -1)], axis=1).astype(BF16)
            for i in range(n_even)]
    w_ukT = [jnp.transpose(b_w_uk[i], (1, 2, 0)).astype(BF16) for i in range(n_even)]
    w_uv = [jnp.transpose(b_w_uv[i], (1, 0, 2)).astype(BF16) for i in range(n_even)]
    w_o_e = w_o_even.astype(BF16)
    w_o_o = w_o_odd.astype(BF16)
    w_up = w_ffn_up.astype(BF16)
    w_down = w_ffn_down.astype(BF16)

    lam_terms = lambda_terms(a_lambda)

    blk = min(ATT_BLOCK, seq)
    n_far = 2
    assert (n_far - 1) * blk + 1 > MAX_DISTANCE or seq // blk <= n_far
    ii = jnp.arange(blk)[:, None]
    jj = jnp.arange(blk)[None, :]
    tiles = jnp.concatenate([t5_bucket(d * blk + ii - jj) for d in range(n_far + 1)], axis=0)
    bias_p = bias_lookup(rel_bias, tiles).reshape(a_heads, n_far + 1, blk, blk)
    q_pos_s = past + jnp.arange(dec_seq)
    k_pos_s = jnp.arange(past + LANE)
    rel_s = jnp.pad(q_pos_s[:, None] - k_pos_s[None, :], ((0, 8 - dec_seq), (0, 0)))
    bias_s = bias_lookup(rel_bias, t5_bucket(rel_s))[:, :dec_seq]
    bias_s = bias_s.reshape(A_KV_HEADS, 1, a_group, dec_seq, -1)
    bias_s = jnp.broadcast_to(bias_s, (A_KV_HEADS, 2, a_group, dec_seq, bias_s.shape[-1]))
    bias_s = bias_s.reshape(A_KV_HEADS * 2 * a_group * dec_seq, -1)

    cos_p, sin_p = _rope_tables(jnp.arange(seq), batch)
    cos_s, sin_s = _rope_tables(q_pos_s, dec_batch)

    ca_k = cache_a_k.reshape(n_even, n_pool, page * A_KV_HEADS, HEAD_DIM)
    ca_v = cache_a_v.reshape(n_even, n_pool, page * A_KV_HEADS, HEAD_DIM)
    cc_k = cache_c_k.reshape(cache_c_k.shape[0], n_pool, page * C_KV_HEADS, HEAD_DIM)
    cc_v = cache_c_v.reshape(cache_c_v.shape[0], n_pool, page * C_KV_HEADS, HEAD_DIM)
    cb_rope_t = jnp.swapaxes(cache_b_rope, 2, 3)
    cc_logf_t = jnp.swapaxes(cache_c_logf, 2, 3)

    def trunk(x, sample):
        nb, t = (dec_batch, dec_seq) if sample else (batch, seq)
        m = nb * t
        x = x.reshape(m, d_model)
        cos, sin = (cos_s, sin_s) if sample else (cos_p, sin_p)
        rows_even, rows_odd = [], []
        for layer in range(depth):
            li = layer // 2
            if layer % 2 == 0:
                lam_init = 0.8 - 0.6 * math.exp(-0.3 * layer)
                lam = lam_terms[li, :1].reshape(1, 1)
                proj = norm_matmul(x, g_mix_pre[layer], w_in_e[li])
                k_a = proj[:, ecols["ak"]:ecols["av"]]
                v_a = proj[:, ecols["av"]:ecols["bq"]]
                qlat, qpe, c_kv, ckv_bf, kpe2, kpe_bf = mla_prep(
                    proj, b_q_norm_g[li], b_kv_norm_g[li], w_uq[li], w_ukT[li], cos, sin, ecols)
                k_pe = kpe2[:, :B_ROPE]
                if sample:
                    q_a = proj[:, :aq_cols].reshape(nb, t, A_KV_HEADS, a_group, 2, a_half)
                    proj3 = proj.reshape(nb, t, -1)
                    oa = attn_a_decode(page_table, lam, _decode_queries(q_a), proj3, ecols, bias_s, a_subln_g[li],
                                       ca_k, ca_v, li, lam_init)
                    a_out = jnp.transpose(oa.reshape(nb, A_KV_HEADS, a_group, t, HEAD_DIM),
                                          (0, 3, 1, 2, 4)).reshape(m, -1).astype(BF16)
                    qlat3 = jnp.transpose(qlat.reshape(nb, t, b_heads, kv_rank).astype(F32),
                                          (0, 2, 1, 3)).reshape(nb, b_heads * t, kv_rank)
                    qpe3 = jnp.transpose(qpe.reshape(nb, t, b_heads, B_ROPE).astype(F32),
                                         (0, 2, 1, 3)).reshape(nb, b_heads * t, B_ROPE)
                    ob = attn_b_decode(page_table, qlat3, qpe3, c_kv.reshape(nb, t, -1),
                                       kpe2.reshape(nb, t, -1), cache_b_latent, cb_rope_t, li, b_scale)
                    lat = jnp.transpose(ob.reshape(nb, b_heads, t, kv_rank), (0, 2, 1, 3)).reshape(m, -1)
                    b_out = head_matmul(lat, w_uv[li])
                else:
                    kv_bf = proj[:, ecols["ak"]:ecols["bq"]].astype(BF16)
                    a_out = attn_a_prompt(lam, proj, kv_bf, bias_p, a_subln_g[li], nb, a_heads,
                                          lam_init, ecols["aq"])
                    b_out = attn_b_prompt(qlat, qpe, ckv_bf, kpe_bf, w_uv[li], nb, b_scale)
                h = jnp.concatenate([a_out, b_out], axis=1)
                x = matmul_norm_residual(h, w_o_e[li], g_mix_post[layer], x)
                rows_even.append((k_a.reshape(nb, t, A_KV_HEADS, HEAD_DIM),
                                  v_a.reshape(nb, t, A_KV_HEADS, HEAD_DIM),
                                  c_kv.reshape(nb, t, kv_rank), k_pe.reshape(nb, t, B_ROPE)))
            else:
                proj = norm_matmul(x, g_mix_pre[layer], w_in_o[li])
                k_c = proj[:, ocols["k"]:ocols["v"]]
                v_c = proj[:, ocols["v"]:ocols["f"]]
                logf_pad, c_pad = logf_cumsum(proj, ocols["f"], c_forget_b[li], nb, cumulate=not sample)
                logf = logf_pad[:, :c_heads]
                if sample:
                    q_c = jnp.transpose(proj[:, :cq_cols].reshape(nb, t, C_KV_HEADS, c_group, HEAD_DIM),
                                        (0, 2, 1, 3, 4)).reshape(nb, C_KV_HEADS, t * c_group, HEAD_DIM)
                    oc = attn_c_decode(page_table, q_c, proj.reshape(nb, t, -1),
                                       logf_pad.reshape(nb, t, LANE), ocols, cc_k, cc_v, cc_logf_t, li)
                    h = jnp.transpose(oc.reshape(nb, C_KV_HEADS, t, c_group, HEAD_DIM),
                                      (0, 2, 1, 3, 4)).reshape(m, -1).astype(BF16)
                else:
                    kv_bf = proj[:, ocols["k"]:ocols["f"]].astype(BF16)
                    c_t = jnp.transpose(c_pad[:, :c_heads].reshape(nb, t, c_heads), (0, 2, 1))
                    h = attn_c_prompt(proj, kv_bf, c_t, nb, c_heads)
                x = matmul_norm_residual(h, w_o_o[li], g_mix_post[layer], x)
                rows_odd.append((k_c.reshape(nb, t, C_KV_HEADS, HEAD_DIM),
                                 v_c.reshape(nb, t, C_KV_HEADS, HEAD_DIM),
                                 logf.reshape(nb, t, c_heads)))
            u = norm_matmul(x, g_ffn_pre[layer], w_up[layer], act=True, out_dtype=BF16, tn=1024)
            x = matmul_norm_residual(u, w_down[layer], g_ffn_post[layer], x)
        even = [jnp.stack(r) for r in zip(*rows_even)]
        odd = [jnp.stack(r) for r in zip(*rows_odd)]
        return x.reshape(nb, t, d_model), even, odd

    y_prompt, (pa_k, pa_v, pb_lat, pb_rope), (pc_k, pc_v, pc_logf) = trunk(x_prompt, False)
    y_sample, (sa_k, sa_v, sb_lat, sb_rope), (sc_k, sc_v, sc_logf) = trunk(x_sample, True)
    return (y_prompt, y_sample, pa_k, sa_k, pa_v, sa_v, pb_lat, sb_lat, pb_rope, sb_rope,
            pc_k, sc_k, pc_v, sc_v, pc_logf, sc_logf)
```

```python
import functools
import math

import jax
import jax.numpy as jnp
from jax import lax
from jax.experimental import pallas as pl
from jax.experimental.pallas import tpu as pltpu

F32 = jnp.float32
BF16 = jnp.bfloat16

HEAD_DIM = 128
A_KV_HEADS = 2
C_KV_HEADS = 2
B_ROPE = 64
N_BUCKETS = 32
MAX_DISTANCE = 128
ROPE_THETA = 10000.0
EPS = 1e-6
A_SCALE = (HEAD_DIM // 2) ** -0.5
assert math.frexp(A_SCALE)[0] == 0.5
NEG = -1e30
LANE = 128
VMEM_LIMIT = 56 * 1024 * 1024

ATT_BLOCK = 512
PAGES_PER_STEP = 16


def _params(*sem):
    return pltpu.CompilerParams(dimension_semantics=sem, vmem_limit_bytes=VMEM_LIMIT)


def _nt_dot(a, b):
    return lax.dot_general(a, b, (((1,), (1,)), ((), ())), preferred_element_type=F32)


def _rms(x, g):
    var = jnp.mean(x * x, axis=-1, keepdims=True)
    return x * lax.rsqrt(var + EPS) * g


def _norm_matmul_kernel(x_ref, g_ref, w_ref, o_ref, hn_ref, *, act):
    @pl.when(pl.program_id(1) == 0)
    def _():
        hn_ref[...] = _rms(x_ref[...], g_ref[...]).astype(BF16)

    y = jnp.dot(hn_ref[...], w_ref[...], preferred_element_type=F32)
    if act:
        y = jnp.square(jnp.maximum(y, 0.0))
    o_ref[...] = y.astype(o_ref.dtype)


def norm_matmul(x, g, w, *, act=False, out_dtype=F32):
    m, d = x.shape
    n = w.shape[1]
    tm = min(m, 1024 if out_dtype == BF16 else 512)
    tn = max(t for t in range(2 * LANE, 10 * LANE + 1, 2 * LANE) if n % t == 0)
    return pl.pallas_call(
        functools.partial(_norm_matmul_kernel, act=act),
        grid=(m // tm, n // tn),
        in_specs=[
            pl.BlockSpec((tm, d), lambda i, j: (i, 0)),
            pl.BlockSpec((1, d), lambda i, j: (0, 0)),
            pl.BlockSpec((d, tn), lambda i, j: (0, j)),
        ],
        out_specs=pl.BlockSpec((tm, tn), lambda i, j: (i, j)),
        out_shape=jax.ShapeDtypeStruct((m, n), out_dtype),
        scratch_shapes=[pltpu.VMEM((tm, d), BF16)],
        compiler_params=_params("parallel", "arbitrary"),
        name="norm_matmul",
    )(x, g.reshape(1, d), w)


def _matmul_norm_res_kernel(h_ref, w_ref, g_ref, x_ref, o_ref, acc_ref):
    k = pl.program_id(1)

    @pl.when(k == 0)
    def _():
        acc_ref[...] = jnp.zeros_like(acc_ref)

    acc_ref[...] += jnp.dot(h_ref[...], w_ref[...], preferred_element_type=F32)

    @pl.when(k == pl.num_programs(1) - 1)
    def _():
        o_ref[...] = x_ref[...] + _rms(acc_ref[...], g_ref[...])


def matmul_norm_residual(h, w, g, x):
    m, kd = h.shape
    d = w.shape[1]
    tm = min(m, 512)
    tk = min(kd, 1024)
    return pl.pallas_call(
        _matmul_norm_res_kernel,
        grid=(m // tm, kd // tk),
        in_specs=[
            pl.BlockSpec((tm, tk), lambda i, k: (i, k)),
            pl.BlockSpec((tk, d), lambda i, k: (k, 0)),
            pl.BlockSpec((1, d), lambda i, k: (0, 0)),
            pl.BlockSpec((tm, d), lambda i, k: (i, 0)),
        ],
        out_specs=pl.BlockSpec((tm, d), lambda i, k: (i, 0)),
        out_shape=jax.ShapeDtypeStruct((m, d), F32),
        scratch_shapes=[pltpu.VMEM((tm, d), F32)],
        compiler_params=_params("parallel", "arbitrary"),
        name="matmul_norm_residual",
    )(h, w, g.reshape(1, d), x)


def _head_matmul_kernel(x_ref, w_ref, o_ref):
    o_ref[...] = jnp.dot(x_ref[...].astype(BF16), w_ref[...],
                         preferred_element_type=F32).astype(o_ref.dtype)


def head_matmul(x, w, *, out_dtype=BF16):
    m = x.shape[0]
    nh_heads, kh, nh = w.shape
    tm = min(m, 512)
    return pl.pallas_call(
        _head_matmul_kernel,
        grid=(m // tm, nh_heads),
        in_specs=[
            pl.BlockSpec((tm, kh), lambda i, h: (i, h)),
            pl.BlockSpec((None, kh, nh), lambda i, h: (h, 0, 0)),
        ],
        out_specs=pl.BlockSpec((tm, nh), lambda i, h: (i, h)),
        out_shape=jax.ShapeDtypeStruct((m, nh_heads * nh), out_dtype),
        compiler_params=_params("parallel", "arbitrary"),
        name="head_matmul",
    )(x, w)


def _rope128(x, cos, sin):
    lane = lax.broadcasted_iota(jnp.int32, x.shape, 1)
    low = (lane % B_ROPE) < (B_ROPE // 2)
    rot = jnp.where(low, -pltpu.roll(x, LANE - B_ROPE // 2, 1), pltpu.roll(x, B_ROPE // 2, 1))
    return x * cos + rot * sin


def _mla_prep_kernel(bq_ref, ckv_ref, kpe_ref, gq_ref, gkv_ref, wuq_ref, wuk_ref, cos_ref, sin_ref,
                     qlat_ref, qpe_ref, ckv_out_ref, ckv_bf_ref, kpe_out_ref, kpe_bf_ref, *, n_heads):
    cos = cos_ref[...]
    sin = sin_ref[...]
    c_q = _rms(bq_ref[...], gq_ref[...]).astype(BF16)
    qb = jnp.dot(c_q, wuq_ref[...], preferred_element_type=F32)
    nope = n_heads * HEAD_DIM
    for h in range(n_heads):
        qn = qb[:, h * HEAD_DIM:(h + 1) * HEAD_DIM].astype(BF16)
        qlat_ref[:, h * 256:(h + 1) * 256] = jnp.dot(
            qn, wuk_ref[h], preferred_element_type=F32).astype(BF16)
    for c in range(n_heads * B_ROPE // LANE):
        x = qb[:, nope + c * LANE: nope + (c + 1) * LANE]
        qpe_ref[:, c * LANE:(c + 1) * LANE] = _rope128(x, cos, sin).astype(BF16)
    c_kv = _rms(ckv_ref[...], gkv_ref[...])
    ckv_out_ref[...] = c_kv
    ckv_bf_ref[...] = c_kv.astype(BF16)
    kr = _rope128(kpe_ref[...], cos, sin)
    lane = lax.broadcasted_iota(jnp.int32, kr.shape, 1)
    kr = jnp.where(lane < B_ROPE, kr, 0.0)
    kd = kr + pltpu.roll(kr, B_ROPE, 1)
    kpe_out_ref[...] = kd
    kpe_bf_ref[...] = kd.astype(BF16)


def mla_prep(proj, gq, gkv, wuq, wukT, cos, sin, cols):
    m = proj.shape[0]
    n_heads, _, kv_rank = wukT.shape
    q_rank = wuq.shape[0]
    tm = min(m, 512)
    assert cols["bq"] % q_rank == 0 and cols["ckv"] % kv_rank == 0 and cols["kpe"] % LANE == 0
    outs = (
        jax.ShapeDtypeStruct((m, n_heads * kv_rank), BF16),
        jax.ShapeDtypeStruct((m, n_heads * B_ROPE), BF16),
        jax.ShapeDtypeStruct((m, kv_rank), F32),
        jax.ShapeDtypeStruct((m, kv_rank), BF16),
        jax.ShapeDtypeStruct((m, LANE), F32),
        jax.ShapeDtypeStruct((m, LANE), BF16),
    )
    row = lambda w: pl.BlockSpec((tm, w), lambda i: (i, 0))
    return pl.pallas_call(
        functools.partial(_mla_prep_kernel, n_heads=n_heads),
        grid=(m // tm,),
        in_specs=[
            pl.BlockSpec((tm, q_rank), lambda i: (i, cols["bq"] // q_rank)),
            pl.BlockSpec((tm, kv_rank), lambda i: (i, cols["ckv"] // kv_rank)),
            pl.BlockSpec((tm, LANE), lambda i: (i, cols["kpe"] // LANE)),
            pl.BlockSpec((1, q_rank), lambda i: (0, 0)),
            pl.BlockSpec((1, kv_rank), lambda i: (0, 0)),
            pl.BlockSpec(wuq.shape, lambda i: (0, 0)),
            pl.BlockSpec(wukT.shape, lambda i: (0, 0, 0)),
            row(LANE), row(LANE),
        ],
        out_specs=[row(n_heads * kv_rank), row(n_heads * B_ROPE), row(kv_rank), row(kv_rank),
                   row(LANE), row(LANE)],
        out_shape=outs,
        compiler_params=_params("parallel"),
        name="mla_prep",
    )(proj, proj, proj, gq.reshape(1, -1), gkv.reshape(1, -1), wuq, wukT, cos, sin)


def _lambda_kernel(a_ref, b_ref, c_ref, d_ref, o_ref):
    s1 = jnp.sum(a_ref[...] * b_ref[...], axis=-1, keepdims=True)
    s2 = jnp.sum(c_ref[...] * d_ref[...], axis=-1, keepdims=True)
    o_ref[...] = jnp.broadcast_to(jnp.exp(s1) - jnp.exp(s2), o_ref.shape)


def lambda_terms(a_lambda):
    n = a_lambda.shape[0]
    parts = [a_lambda[:, i, :] for i in range(4)]
    return pl.pallas_call(
        _lambda_kernel,
        out_shape=jax.ShapeDtypeStruct((n, LANE), F32),
        name="lambda_terms",
    )(*parts)


def _bias_lookup_kernel(tab_ref, bucket_ref, o_ref):
    h = pl.program_id(0)
    bucket = bucket_ref[...]
    out = jnp.full(bucket.shape, tab_ref[h, N_BUCKETS - 1], F32)
    for b in range(N_BUCKETS - 1):
        out = jnp.where(bucket == b, tab_ref[h, b], out)
    o_ref[...] = out


def bias_lookup(rel_bias, bucket):
    n_heads = rel_bias.shape[1]
    r, s = bucket.shape
    return pl.pallas_call(
        _bias_lookup_kernel,
        grid=(n_heads,),
        in_specs=[
            pl.BlockSpec(memory_space=pltpu.SMEM),
            pl.BlockSpec((r, s), lambda h: (0, 0)),
        ],
        out_specs=pl.BlockSpec((None, r, s), lambda h: (h, 0, 0)),
        out_shape=jax.ShapeDtypeStruct((n_heads, r, s), F32),
        compiler_params=_params("arbitrary"),
        name="bias_lookup",
    )(rel_bias.T, bucket)


def t5_bucket(rel):
    n = jnp.maximum(rel, 0)
    max_exact = N_BUCKETS // 2
    nf = jnp.maximum(n, 1).astype(F32)
    large = max_exact + (jnp.log(nf / max_exact) / math.log(MAX_DISTANCE / max_exact)
                         * (N_BUCKETS - max_exact)).astype(jnp.int32)
    return jnp.where(n < max_exact, n, jnp.minimum(large, N_BUCKETS - 1))


def _split3(x):
    x1 = x.astype(BF16)
    r1 = x - x1.astype(F32)
    x2 = r1.astype(BF16)
    r2 = r1 - x2.astype(F32)
    return x1, x2, r2.astype(BF16)


def _logf_cumsum_kernel(f_ref, b_ref, logf_ref, c_ref, carry_ref, *, cumulate):
    x = f_ref[...] + b_ref[...]
    logf = jnp.minimum(x, 0.0) - jnp.log1p(jnp.exp(-jnp.abs(x)))
    logf_ref[...] = logf
    if cumulate:
        @pl.when(pl.program_id(1) == 0)
        def _():
            carry_ref[...] = jnp.zeros_like(carry_ref)

        t = logf.shape[0]
        tri = (lax.broadcasted_iota(jnp.int32, (t, t), 0)
               >= lax.broadcasted_iota(jnp.int32, (t, t), 1)).astype(BF16)
        c = carry_ref[...]
        for piece in _split3(logf):
            c = c + jnp.dot(tri, piece, preferred_element_type=F32)
        c_ref[...] = c
        carry_ref[...] = c[t - 1:t, :]
    else:
        c_ref[...] = logf


def logf_cumsum(proj, col, b_f, batch, *, cumulate):
    m = proj.shape[0]
    t = m // batch
    tt = min(t, 256) if cumulate else min(m, 512)
    nb = t // tt if cumulate else m // tt
    grid = (batch, nb) if cumulate else (1, nb)
    bpad = jnp.zeros((1, LANE), F32).at[0, :b_f.shape[0]].set(b_f)
    spec = pl.BlockSpec((tt, LANE), lambda b, i: (b * nb + i, 0))
    return pl.pallas_call(
        functools.partial(_logf_cumsum_kernel, cumulate=cumulate),
        grid=grid,
        in_specs=[
            pl.BlockSpec((tt, LANE), lambda b, i: (b * nb + i, col // LANE)),
            pl.BlockSpec((1, LANE), lambda b, i: (0, 0)),
        ],
        out_specs=[spec, spec],
        out_shape=(jax.ShapeDtypeStruct((m, LANE), F32), jax.ShapeDtypeStruct((m, LANE), F32)),
        scratch_shapes=[pltpu.VMEM((1, LANE), F32)],
        compiler_params=_params("arbitrary", "arbitrary"),
        name="logf_cumsum",
    )(proj, bpad)


def _online_update(state, s, v):
    m, l, acc = state
    m_new = jnp.maximum(m, jnp.max(s, axis=-1, keepdims=True))
    alpha = jnp.exp(m - m_new)
    p = jnp.exp(s - m_new)
    l = alpha * l + jnp.sum(p, axis=-1, keepdims=True)
    acc = alpha * acc + jnp.dot(p.astype(BF16), v, preferred_element_type=F32)
    return m_new, l, acc


def _init_state(rows, width):
    return (jnp.full((rows, 1), NEG, F32), jnp.zeros((rows, 1), F32), jnp.zeros((rows, width), F32))


def _causal_sweep(step, init, qi):
    state = lax.fori_loop(0, qi, lambda j, st: step(j, st, False), init)
    return step(qi, state, True)


def _local_causal(blk):
    return (lax.broadcasted_iota(jnp.int32, (blk, blk), 1)
            <= lax.broadcasted_iota(jnp.int32, (blk, blk), 0))


def _attn_a_kernel(lam_ref, q_ref, k_ref, v_ref, bias_ref, g_ref, o_ref, *, blk, lam_init, n_far):
    qi = pl.program_id(2)
    q = q_ref[...] * A_SCALE
    lane = lax.broadcasted_iota(jnp.int32, q.shape, 1)
    qc = (jnp.where(lane < HEAD_DIM // 2, q, 0.0).astype(BF16),
          jnp.where(lane >= HEAD_DIM // 2, q, 0.0).astype(BF16))
    causal = _local_causal(blk)

    def step(j, state, masked):
        start = pl.multiple_of(j * blk, blk)
        k = k_ref[pl.ds(start, blk), :]
        v = v_ref[pl.ds(start, blk), :]
        bias = bias_ref[jnp.minimum(qi - j, n_far)]
        out = []
        for c in range(2):
            s = _nt_dot(qc[c], k) + bias
            if masked:
                s = jnp.where(causal, s, NEG)
            out.append(_online_update(state[c], s, v))
        return tuple(out)

    init = (_init_state(blk, HEAD_DIM), _init_state(blk, HEAD_DIM))
    (m0, l0, a0), (m1, l1, a1) = _causal_sweep(step, init, qi)
    lam = lam_ref[0, 0] + lam_init
    o = a0 / l0 - lam * (a1 / l1)
    o_ref[...] = (_rms(o, g_ref[...]) * (1.0 - lam_init)).astype(o_ref.dtype)


def attn_a_prompt(lam, proj, kv_bf, bias, subln_g, batch, n_heads, lam_init, qcol):
    m = proj.shape[0]
    t = m // batch
    blk = min(ATT_BLOCK, t)
    nq = t // blk
    group = n_heads // A_KV_HEADS
    n_far = bias.shape[1] - 1
    return pl.pallas_call(
        functools.partial(_attn_a_kernel, blk=blk, lam_init=lam_init, n_far=n_far),
        grid=(batch, n_heads, nq),
        in_specs=[
            pl.BlockSpec(memory_space=pltpu.SMEM),
            pl.BlockSpec((blk, HEAD_DIM), lambda b, h, i: (b * nq + i, qcol // HEAD_DIM + h)),
            pl.BlockSpec((t, HEAD_DIM), lambda b, h, i: (b, h // group)),
            pl.BlockSpec((t, HEAD_DIM), lambda b, h, i: (b, A_KV_HEADS + h // group)),
            pl.BlockSpec((None,) + bias.shape[1:], lambda b, h, i: (h, 0, 0, 0)),
            pl.BlockSpec((1, HEAD_DIM), lambda b, h, i: (0, 0)),
        ],
        out_specs=pl.BlockSpec((blk, HEAD_DIM), lambda b, h, i: (b * nq + i, h)),
        out_shape=jax.ShapeDtypeStruct((m, n_heads * HEAD_DIM), BF16),
        compiler_params=_params("parallel", "arbitrary", "arbitrary"),
        name="attn_a_prompt",
    )(lam, proj, kv_bf, kv_bf, bias, subln_g.reshape(1, -1))


def _attn_b_kernel(qlat_ref, qpe_ref, ckv_ref, kpe_ref, wuv_ref, o_ref, *, blk, scale, hp):
    qi = pl.program_id(2)
    kv_rank = ckv_ref.shape[1]
    dv = wuv_ref.shape[2]
    qpe = qpe_ref[...]
    lane = lax.broadcasted_iota(jnp.int32, qpe.shape, 1)
    qlat = [qlat_ref[:, u * kv_rank:(u + 1) * kv_rank] for u in range(hp)]
    qrope = [jnp.where((lane // B_ROPE) == u, qpe, jnp.zeros_like(qpe)) for u in range(hp)]
    causal = _local_causal(blk)

    def step(j, state, masked):
        start = pl.multiple_of(j * blk, blk)
        ckv = ckv_ref[pl.ds(start, blk), :]
        kpe = kpe_ref[pl.ds(start, blk), :]
        out = []
        for u in range(hp):
            s = (_nt_dot(qlat[u], ckv) + _nt_dot(qrope[u], kpe)) * scale
            if masked:
                s = jnp.where(causal, s, NEG)
            out.append(_online_update(state[u], s, ckv))
        return tuple(out)

    final = _causal_sweep(step, tuple(_init_state(blk, kv_rank) for _ in range(hp)), qi)
    for u, (m, l, acc) in enumerate(final):
        lat = (acc / l).astype(BF16)
        o_ref[:, u * dv:(u + 1) * dv] = jnp.dot(lat, wuv_ref[u],
                                                preferred_element_type=F32).astype(o_ref.dtype)


def attn_b_prompt(qlat, qpe, ckv_bf, kpe_bf, wuv, batch, scale):
    m = qlat.shape[0]
    n_heads, kv_rank, dv = wuv.shape
    t = m // batch
    blk = min(ATT_BLOCK, t)
    nq = t // blk
    hp = LANE // B_ROPE
    return pl.pallas_call(
        functools.partial(_attn_b_kernel, blk=blk, scale=scale, hp=hp),
        grid=(batch, n_heads // hp, nq),
        in_specs=[
            pl.BlockSpec((blk, hp * kv_rank), lambda b, h, i: (b * nq + i, h)),
            pl.BlockSpec((blk, LANE), lambda b, h, i: (b * nq + i, h)),
            pl.BlockSpec((t, kv_rank), lambda b, h, i: (b, 0)),
            pl.BlockSpec((t, LANE), lambda b, h, i: (b, 0)),
            pl.BlockSpec((hp, kv_rank, dv), lambda b, h, i: (h, 0, 0)),
        ],
        out_specs=pl.BlockSpec((blk, hp * dv), lambda b, h, i: (b * nq + i, h)),
        out_shape=jax.ShapeDtypeStruct((m, n_heads * dv), BF16),
        compiler_params=_params("parallel", "arbitrary", "arbitrary"),
        name="attn_b_prompt",
    )(qlat, qpe, ckv_bf, kpe_bf, wuv)


def _attn_c_kernel(q_ref, k_ref, v_ref, c_ref, o_ref, *, blk, hp):
    hq = pl.program_id(1)
    qi = pl.program_id(2)
    scale = HEAD_DIM ** -0.5
    causal = _local_causal(blk)
    qstart = pl.multiple_of(qi * blk, blk)
    eye = (lax.broadcasted_iota(jnp.int32, (blk, blk), 0)
           == lax.broadcasted_iota(jnp.int32, (blk, blk), 1))
    q, cq = [], []
    for u in range(hp):
        q.append((q_ref[:, u * HEAD_DIM:(u + 1) * HEAD_DIM] * scale).astype(BF16))
        cq_row = c_ref[pl.ds(hq * hp + u, 1), pl.ds(qstart, blk)]
        cq.append(jnp.sum(jnp.where(eye, cq_row, 0.0), axis=-1, keepdims=True))

    def step(j, state, masked):
        start = pl.multiple_of(j * blk, blk)
        k = k_ref[pl.ds(start, blk), :]
        v = v_ref[pl.ds(start, blk), :]
        out = []
        for u in range(hp):
            ck = c_ref[pl.ds(hq * hp + u, 1), pl.ds(start, blk)]
            s = _nt_dot(q[u], k) + (cq[u] - ck)
            if masked:
                s = jnp.where(causal, s, NEG)
            out.append(_online_update(state[u], s, v))
        return tuple(out)

    final = _causal_sweep(step, tuple(_init_state(blk, HEAD_DIM) for _ in range(hp)), qi)
    for u, (m, l, acc) in enumerate(final):
        o_ref[:, u * HEAD_DIM:(u + 1) * HEAD_DIM] = (acc / l).astype(o_ref.dtype)


def attn_c_prompt(proj, kv_bf, c_t, batch, n_heads, hp=2):
    m = proj.shape[0]
    t = m // batch
    blk = min(ATT_BLOCK, t)
    nq = t // blk
    group = n_heads // C_KV_HEADS
    assert group % hp == 0
    return pl.pallas_call(
        functools.partial(_attn_c_kernel, blk=blk, hp=hp),
        grid=(batch, n_heads // hp, nq),
        in_specs=[
            pl.BlockSpec((blk, hp * HEAD_DIM), lambda b, h, i: (b * nq + i, h)),
            pl.BlockSpec((t, HEAD_DIM), lambda b, h, i: (b, h * hp // group)),
            pl.BlockSpec((t, HEAD_DIM), lambda b, h, i: (b, C_KV_HEADS + h * hp // group)),
            pl.BlockSpec((None, n_heads, t), lambda b, h, i: (b, 0, 0)),
        ],
        out_specs=pl.BlockSpec((blk, hp * HEAD_DIM), lambda b, h, i: (b * nq + i, h)),
        out_shape=jax.ShapeDtypeStruct((m, n_heads * HEAD_DIM), BF16),
        compiler_params=_params("parallel", "arbitrary", "arbitrary"),
        name="attn_c_prompt",
    )(proj, kv_bf, kv_bf, c_t)


def _scratch_update(m_ref, l_ref, acc_ref, s, v_list):
    m = m_ref[...]
    m_new = jnp.maximum(m, jnp.max(s, axis=-1, keepdims=True))
    alpha = jnp.exp(m - m_new)
    p = jnp.exp(s - m_new)
    l_ref[...] = alpha * l_ref[...] + jnp.sum(p, axis=-1, keepdims=True)
    pb = p.astype(BF16)
    per = s.shape[0] // len(v_list)
    pv = [jnp.dot(pb[i * per:(i + 1) * per], v, preferred_element_type=F32) for i, v in enumerate(v_list)]
    acc_ref[...] = alpha * acc_ref[...] + (pv[0] if len(pv) == 1 else jnp.concatenate(pv, axis=0))
    m_ref[...] = m_new


def _head_rows(ref, kh, n_keys, n_kv):
    return ref[pl.ds(kh, n_keys, stride=n_kv), :]


def _rows_dst(buf, i):
    n = buf.shape[0] // PAGES_PER_STEP
    return buf.at[pl.ds(i * n, n)]


def _lanes_dst(buf, i):
    n = buf.shape[1] // PAGES_PER_STEP
    return buf.at[:, pl.ds(i * n, n)]


def _page_copies(pt_ref, streams, sem, seq, grp, slot, *, li, n_pages, reverse):
    out = []
    for i in range(PAGES_PER_STEP):
        logical = grp * PAGES_PER_STEP + i
        if reverse:
            logical = n_pages - 1 - logical
        phys = pt_ref[seq * n_pages + logical]
        for a, (hbm, buf, dst) in enumerate(streams):
            out.append(pltpu.make_async_copy(hbm.at[li, phys], dst(buf.at[slot], i), sem.at[a, slot]))
    return out


def _fetch_pages(pt_ref, streams, sem, **kw):
    b, g = pl.program_id(0), pl.program_id(1)
    nb, ng = pl.num_programs(0), pl.num_programs(1)
    step = b * ng + g
    slot = step % 2
    copies = functools.partial(_page_copies, pt_ref, streams, sem, **kw)

    @pl.when(step == 0)
    def _():
        for c in copies(b, g, slot):
            c.start()

    @pl.when(step + 1 < nb * ng)
    def _():
        nxt = step + 1
        for c in copies(nxt // ng, nxt % ng, 1 - slot):
            c.start()

    for c in copies(b, g, slot):
        c.wait()
    return slot


def _stack(parts):
    return parts[0] if len(parts) == 1 else jnp.concatenate(parts, axis=0)


def _scratch_update_row(m_ref, l_ref, acc_ref, s_col, v_row):
    m = m_ref[...]
    m_new = jnp.maximum(m, s_col)
    alpha = jnp.exp(m - m_new)
    p = jnp.exp(s_col - m_new)
    l_ref[...] = alpha * l_ref[...] + p
    acc_ref[...] = alpha * acc_ref[...] + p * v_row
    m_ref[...] = m_new


def _init_scratch(m_ref, l_ref, acc_ref):
    m_ref[...] = jnp.full(m_ref.shape, NEG, F32)
    l_ref[...] = jnp.zeros(l_ref.shape, F32)
    acc_ref[...] = jnp.zeros(acc_ref.shape, F32)


def _dec_a_kernel(pt_ref, lam_ref, q_ref, kn_ref, vn_ref, bias_ref, g_ref, k_hbm, v_hbm,
                  o_ref, kbuf, vbuf, sem, m_ref, l_ref, acc_ref, *, li, page, n_pages, n_new, lam_init):
    g = pl.program_id(1)
    scale = (HEAD_DIM // 2) ** -0.5
    q = q_ref[...]
    n_kv, per, _ = q.shape
    rows = n_kv * per
    nk = PAGES_PER_STEP * page
    past = n_pages * page
    slot = _fetch_pages(pt_ref, [(k_hbm, kbuf, _rows_dst), (v_hbm, vbuf, _rows_dst)], sem,
                        li=li, n_pages=n_pages, reverse=False)

    @pl.when(g == 0)
    def _():
        _init_scratch(m_ref, l_ref, acc_ref)

    q_bf = q.astype(BF16)
    kb, vb = kbuf.at[slot], vbuf.at[slot]
    start = pl.multiple_of(g * nk, nk)
    s = _stack([_nt_dot(q_bf[kh], _head_rows(kb, kh, nk, n_kv).astype(BF16)) for kh in range(n_kv)])
    s = s * scale + bias_ref[:, pl.ds(start, nk)]
    _scratch_update(m_ref, l_ref, acc_ref, s,
                    [_head_rows(vb, kh, nk, n_kv).astype(BF16) for kh in range(n_kv)])

    @pl.when(g == pl.num_programs(1) - 1)
    def _():
        t_row = lax.broadcasted_iota(jnp.int32, (rows, 1), 0) % n_new
        kn = kn_ref[...]
        vn = vn_ref[...]
        for j in range(n_new):
            s = _stack([jnp.sum(q[kh] * kn[j:j + 1, kh * HEAD_DIM:(kh + 1) * HEAD_DIM],
                                axis=-1, keepdims=True) for kh in range(n_kv)]) * scale
            s = s + bias_ref[:, past + j:past + j + 1]
            s = jnp.where(t_row >= j, s, NEG)
            v_rows = _stack([jnp.broadcast_to(vn[j:j + 1, kh * HEAD_DIM:(kh + 1) * HEAD_DIM],
                                              (per, HEAD_DIM)) for kh in range(n_kv)])
            _scratch_update_row(m_ref, l_ref, acc_ref, s, v_rows)
        o = acc_ref[...] / l_ref[...]
        lam = lam_ref[0, 0] + lam_init
        half = per // 2
        for kh in range(n_kv):
            base = kh * per
            d = o[base:base + half] - lam * o[base + half:base + per]
            o_ref[kh] = _rms(d, g_ref[...]) * (1.0 - lam_init)


def attn_a_decode(page_table, lam, q4, proj3, cols, bias, subln_g, cache_k, cache_v, li, lam_init):
    nb, n_pages = page_table.shape
    n_kv, per = q4.shape[1], q4.shape[2]
    page = cache_k.shape[2] // n_kv
    width = n_kv * HEAD_DIM
    rows = n_kv * per
    n_new = proj3.shape[1]
    pg = PAGES_PER_STEP
    assert n_pages % pg == 0
    ng = n_pages // pg
    page_buf = pltpu.VMEM((2, pg * page * n_kv, HEAD_DIM), F32)

    in_specs = [
        pl.BlockSpec(memory_space=pltpu.SMEM),
        pl.BlockSpec((None, n_kv, per, HEAD_DIM), lambda b, g, pt: (b, 0, 0, 0)),
        pl.BlockSpec((None, n_new, width), lambda b, g, pt: (b, 0, cols["ak"] // width)),
        pl.BlockSpec((None, n_new, width), lambda b, g, pt: (b, 0, cols["av"] // width)),
        pl.BlockSpec(bias.shape, lambda b, g, pt: (0, 0)),
        pl.BlockSpec((1, HEAD_DIM), lambda b, g, pt: (0, 0)),
        pl.BlockSpec(memory_space=pl.ANY),
        pl.BlockSpec(memory_space=pl.ANY),
    ]
    return pl.pallas_call(
        functools.partial(_dec_a_kernel, li=li, page=page, n_pages=n_pages, n_new=n_new,
                          lam_init=lam_init),
        grid_spec=pltpu.PrefetchScalarGridSpec(
            num_scalar_prefetch=1,
            grid=(nb, ng),
            in_specs=in_specs,
            out_specs=pl.BlockSpec((None, n_kv, per // 2, HEAD_DIM), lambda b, g, pt: (b, 0, 0, 0)),
            scratch_shapes=[page_buf, page_buf, pltpu.SemaphoreType.DMA((2, 2)),
                            pltpu.VMEM((rows, 1), F32), pltpu.VMEM((rows, 1), F32),
                            pltpu.VMEM((rows, HEAD_DIM), F32)],
        ),
        out_shape=jax.ShapeDtypeStruct((nb, n_kv, per // 2, HEAD_DIM), F32),
        compiler_params=_params("arbitrary", "arbitrary"),
        name="attn_a_decode",
    )(page_table.reshape(-1), lam, q4, proj3, proj3, bias, subln_g.reshape(1, -1), cache_k, cache_v)


def _dec_b_kernel(pt_ref, qlat_ref, qpe_ref, cn_ref, kn_ref, lat_hbm, rope_hbm,
                  o_ref, latbuf, ropebuf, sem, m_ref, l_ref, acc_ref, *, li, n_pages, n_new, scale):
    g = pl.program_id(1)
    qlat = qlat_ref[...]
    qpe = qpe_ref[...]
    rows = qlat.shape[0]
    slot = _fetch_pages(pt_ref, [(lat_hbm, latbuf, _rows_dst), (rope_hbm, ropebuf, _lanes_dst)], sem,
                        li=li, n_pages=n_pages, reverse=False)

    @pl.when(g == 0)
    def _():
        _init_scratch(m_ref, l_ref, acc_ref)

    lat = latbuf[slot].astype(BF16)
    s = (_nt_dot(qlat.astype(BF16), lat)
         + jnp.dot(qpe.astype(BF16), ropebuf[slot].astype(BF16), preferred_element_type=F32)) * scale
    _scratch_update(m_ref, l_ref, acc_ref, s, [lat])

    @pl.when(g == pl.num_programs(1) - 1)
    def _():
        t_row = lax.broadcasted_iota(jnp.int32, (rows, 1), 0) % n_new
        cn = cn_ref[...]
        kn = kn_ref[...][:, :B_ROPE]
        for j in range(n_new):
            s = (jnp.sum(qlat * cn[j:j + 1, :], axis=-1, keepdims=True)
                 + jnp.sum(qpe * kn[j:j + 1, :], axis=-1, keepdims=True)) * scale
            s = jnp.where(t_row >= j, s, NEG)
            _scratch_update_row(m_ref, l_ref, acc_ref, s, cn[j:j + 1, :])
        o_ref[...] = acc_ref[...] / l_ref[...]


def attn_b_decode(page_table, qlat3, qpe3, ckv3, kpe3, cache_lat, cache_rope_t, li, scale):
    nb, n_pages = page_table.shape
    page = cache_lat.shape[2]
    kv_rank = cache_lat.shape[3]
    rows = qlat3.shape[1]
    n_new = ckv3.shape[1]
    pg = PAGES_PER_STEP
    assert n_pages % pg == 0
    ng = n_pages // pg

    in_specs = [
        pl.BlockSpec((None, rows, kv_rank), lambda b, g, pt: (b, 0, 0)),
        pl.BlockSpec((None, rows, B_ROPE), lambda b, g, pt: (b, 0, 0)),
        pl.BlockSpec((None, n_new, kv_rank), lambda b, g, pt: (b, 0, 0)),
        pl.BlockSpec((None, n_new, LANE), lambda b, g, pt: (b, 0, 0)),
        pl.BlockSpec(memory_space=pl.ANY),
        pl.BlockSpec(memory_space=pl.ANY),
    ]
    return pl.pallas_call(
        functools.partial(_dec_b_kernel, li=li, n_pages=n_pages, n_new=n_new, scale=scale),
        grid_spec=pltpu.PrefetchScalarGridSpec(
            num_scalar_prefetch=1,
            grid=(nb, ng),
            in_specs=in_specs,
            out_specs=pl.BlockSpec((None, rows, kv_rank), lambda b, g, pt: (b, 0, 0)),
            scratch_shapes=[pltpu.VMEM((2, pg * page, kv_rank), F32),
                            pltpu.VMEM((2, B_ROPE, pg * page), F32),
                            pltpu.SemaphoreType.DMA((2, 2)),
                            pltpu.VMEM((rows, 1), F32), pltpu.VMEM((rows, 1), F32),
                            pltpu.VMEM((rows, kv_rank), F32)],
        ),
        out_shape=jax.ShapeDtypeStruct((nb, rows, kv_rank), F32),
        compiler_params=_params("arbitrary", "arbitrary"),
        name="attn_b_decode",
    )(page_table.reshape(-1), qlat3, qpe3, ckv3, kpe3, cache_lat, cache_rope_t)


def _dec_c_kernel(pt_ref, q_ref, kn_ref, vn_ref, fn_ref, k_hbm, v_hbm, f_hbm,
                  o_ref, kbuf, vbuf, fbuf, sem, m_ref, l_ref, acc_ref, later_ref, newdecay_ref,
                  *, li, page, n_pages, n_new, n_heads):
    g = pl.program_id(1)
    scale = HEAD_DIM ** -0.5
    q = q_ref[...]
    n_kv, per, _ = q.shape
    rows = n_kv * per
    group = n_heads // n_kv
    nk = PAGES_PER_STEP * page
    row = lax.broadcasted_iota(jnp.int32, (rows, 1), 0)
    t_row = (row % per) // group
    head_row = (row // per) * group + row % group
    expand = head_row == lax.broadcasted_iota(jnp.int32, (rows, n_heads), 1)
    slot = _fetch_pages(pt_ref, [(k_hbm, kbuf, _rows_dst), (v_hbm, vbuf, _rows_dst),
                                 (f_hbm, fbuf, _rows_dst)], sem,
                        li=li, n_pages=n_pages, reverse=True)

    def new_logf_col(j):
        return jnp.sum(jnp.where(expand, fn_ref[...][j:j + 1, :n_heads], 0.0), axis=-1, keepdims=True)

    @pl.when(g == 0)
    def _():
        _init_scratch(m_ref, l_ref, acc_ref)
        kn = kn_ref[...]
        vn = vn_ref[...]
        cols = [new_logf_col(j) for j in range(n_new)]
        total = jnp.zeros((rows, 1), F32)
        for j in range(n_new):
            total = total + jnp.where(t_row >= j, cols[j], 0.0)
        newdecay_ref[...] = total
        later_ref[...] = jnp.zeros(later_ref.shape, F32)
        for j in range(n_new):
            decay = jnp.zeros((rows, 1), F32)
            for i in range(j + 1, n_new):
                decay = decay + jnp.where(t_row >= i, cols[i], 0.0)
            s = _stack([jnp.sum(q[kh] * kn[j:j + 1, kh * HEAD_DIM:(kh + 1) * HEAD_DIM],
                                axis=-1, keepdims=True) for kh in range(n_kv)]) * scale + decay
            s = jnp.where(t_row >= j, s, NEG)
            v_rows = _stack([jnp.broadcast_to(vn[j:j + 1, kh * HEAD_DIM:(kh + 1) * HEAD_DIM],
                                              (per, HEAD_DIM)) for kh in range(n_kv)])
            _scratch_update_row(m_ref, l_ref, acc_ref, s, v_rows)

    q_bf = q.astype(BF16)
    kb, vb = kbuf.at[slot], vbuf.at[slot]
    f = fbuf[slot]
    after = (lax.broadcasted_iota(jnp.int32, (page, page), 0)
             > lax.broadcasted_iota(jnp.int32, (page, page), 1)).astype(BF16)
    within = jnp.zeros(f.shape, F32)
    for piece in _split3(f):
        within = within + jnp.dot(piece, after, preferred_element_type=F32)
    page_sum = jnp.sum(f, axis=-1, keepdims=True)
    later = later_ref[...]
    per_page = []
    for i in range(PAGES_PER_STEP):
        sl = slice(i * n_heads, (i + 1) * n_heads)
        per_page.append(within[sl] + later)
        later = later + page_sum[sl]
    later_ref[...] = later
    by_head = jnp.concatenate(per_page, axis=1)
    decay = _stack([jnp.concatenate([by_head[kh * group:(kh + 1) * group]] * n_new, axis=0)
                    for kh in range(n_kv)])
    s = _stack([_nt_dot(q_bf[kh], _head_rows(kb, kh, nk, n_kv).astype(BF16)) for kh in range(n_kv)])
    s = s * scale + (decay + newdecay_ref[...])
    _scratch_update(m_ref, l_ref, acc_ref, s,
                    [_head_rows(vb, kh, nk, n_kv).astype(BF16) for kh in range(n_kv)])

    @pl.when(g == pl.num_programs(1) - 1)
    def _():
        o = acc_ref[...] / l_ref[...]
        for kh in range(n_kv):
            o_ref[kh] = o[kh * per:(kh + 1) * per]


def attn_c_decode(page_table, q4, proj3, logf3, cols, cache_k, cache_v, cache_f_t, li):
    nb, n_pages = page_table.shape
    n_kv, per = q4.shape[1], q4.shape[2]
    page = cache_k.shape[2] // n_kv
    width = n_kv * HEAD_DIM
    n_heads = cache_f_t.shape[2]
    rows = n_kv * per
    n_new = proj3.shape[1]
    pg = PAGES_PER_STEP
    assert n_pages % pg == 0
    ng = n_pages // pg
    page_buf = pltpu.VMEM((2, pg * page * n_kv, HEAD_DIM), F32)

    in_specs = [
        pl.BlockSpec((None, n_kv, per, HEAD_DIM), lambda b, g, pt: (b, 0, 0, 0)),
        pl.BlockSpec((None, n_new, width), lambda b, g, pt: (b, 0, cols["k"] // width)),
        pl.BlockSpec((None, n_new, width), lambda b, g, pt: (b, 0, cols["v"] // width)),
        pl.BlockSpec((None, n_new, LANE), lambda b, g, pt: (b, 0, 0)),
        pl.BlockSpec(memory_space=pl.ANY),
        pl.BlockSpec(memory_space=pl.ANY),
        pl.BlockSpec(memory_space=pl.ANY),
    ]
    return pl.pallas_call(
        functools.partial(_dec_c_kernel, li=li, page=page, n_pages=n_pages, n_new=n_new, n_heads=n_heads),
        grid_spec=pltpu.PrefetchScalarGridSpec(
            num_scalar_prefetch=1,
            grid=(nb, ng),
            in_specs=in_specs,
            out_specs=pl.BlockSpec((None, n_kv, per, HEAD_DIM), lambda b, g, pt: (b, 0, 0, 0)),
            scratch_shapes=[page_buf, page_buf, pltpu.VMEM((2, pg * n_heads, page), F32),
                            pltpu.SemaphoreType.DMA((3, 2)),
                            pltpu.VMEM((rows, 1), F32), pltpu.VMEM((rows, 1), F32),
                            pltpu.VMEM((rows, HEAD_DIM), F32),
                            pltpu.VMEM((n_heads, 1), F32), pltpu.VMEM((rows, 1), F32)],
        ),
        out_shape=jax.ShapeDtypeStruct((nb, n_kv, per, HEAD_DIM), F32),
        compiler_params=_params("arbitrary", "arbitrary"),
        name="attn_c_decode",
    )(page_table.reshape(-1), q4, proj3, proj3, logf3, cache_k, cache_v, cache_f_t)


def _pad_cols(w, n):
    return jnp.pad(w, ((0, 0), (0, n - w.shape[1])))


def _rope_tables(pos, reps):
    half = B_ROPE // 2
    inv = ROPE_THETA ** (-jnp.arange(half, dtype=F32) / half)
    ang = pos.astype(F32)[:, None] * inv[None, :]
    cos = jnp.tile(jnp.cos(ang), (reps, LANE // half))
    sin = jnp.tile(jnp.sin(ang), (reps, LANE // half))
    return cos, sin


def _decode_queries(q):
    b, t, n_kv, group, n_maps, w = q.shape
    qp = jnp.transpose(q, (0, 2, 4, 3, 1, 5))
    out = jnp.zeros((b, n_kv, n_maps, group, t, n_maps, w), q.dtype)
    for c in range(n_maps):
        out = out.at[:, :, c, :, :, c, :].set(qp[:, :, c])
    return out.reshape(b, n_kv, n_maps * group * t, n_maps * w)


def kernel(x_prompt, x_sample, cache_a_k, cache_a_v, cache_b_latent, cache_b_rope, cache_c_k, cache_c_v, cache_c_logf, page_table, rel_bias, w_in_even, a_lambda, a_subln_g, b_q_norm_g, b_w_uq, b_kv_norm_g, b_w_uk, b_w_uv, w_o_even, w_in_odd, c_forget_b, w_o_odd, g_mix_pre, g_mix_post, g_ffn_pre, g_ffn_post, w_ffn_up, w_ffn_down):
    batch, seq, d_model = x_prompt.shape
    dec_batch, dec_seq, _ = x_sample.shape
    depth = g_mix_pre.shape[0]
    n_even = w_in_even.shape[0]
    n_pool, page = cache_a_k.shape[1], cache_a_k.shape[2]
    n_pages = page_table.shape[1]
    past = n_pages * page
    a_heads = rel_bias.shape[1]
    a_group = a_heads // A_KV_HEADS
    a_half = HEAD_DIM // 2
    b_heads = b_w_uq.shape[2]
    q_rank = b_w_uq.shape[1]
    kv_rank = b_w_uk.shape[1]
    b_nope = b_w_uk.shape[3]
    c_heads = c_forget_b.shape[1]
    c_group = c_heads // C_KV_HEADS
    b_scale = (b_nope + B_ROPE) ** -0.5

    aq_cols = a_heads * HEAD_DIM
    ak_cols = A_KV_HEADS * HEAD_DIM
    ecols = {"aq": 0, "ak": aq_cols, "av": aq_cols + ak_cols, "bq": aq_cols + 2 * ak_cols}
    ecols["ckv"] = ecols["bq"] + q_rank
    ecols["kpe"] = ecols["ckv"] + kv_rank
    even_in = ecols["kpe"] + B_ROPE
    even_pad = -(-even_in // 512) * 512
    cq_cols = c_heads * HEAD_DIM
    ckv_cols = C_KV_HEADS * HEAD_DIM
    ocols = {"q": 0, "k": cq_cols, "v": cq_cols + ckv_cols, "f": cq_cols + 2 * ckv_cols}
    odd_in = ocols["f"] + c_heads
    odd_pad = -(-(ocols["f"] + LANE) // 512) * 512

    w_in_e = [_pad_cols(w_in_even[i], even_pad).astype(BF16) for i in range(n_even)]
    w_in_o = [_pad_cols(w_in_odd[i], odd_pad).astype(BF16) for i in range(w_in_odd.shape[0])]
    w_uq = [jnp.concatenate([b_w_uq[i][:, :, :b_nope].reshape(q_rank, -1),
                             b_w_uq[i][:, :, b_nope:].reshape(q_rank, -1)], axis=1).astype(BF16)
            for i in range(n_even)]
    w_ukT = [jnp.transpose(b_w_uk[i], (1, 2, 0)).astype(BF16) for i in range(n_even)]
    w_uv = [jnp.transpose(b_w_uv[i], (1, 0, 2)).astype(BF16) for i in range(n_even)]
    w_o_e = w_o_even.astype(BF16)
    w_o_o = w_o_odd.astype(BF16)
    w_up = w_ffn_up.astype(BF16)
    w_down = w_ffn_down.astype(BF16)

    lam_terms = lambda_terms(a_lambda)

    blk = min(ATT_BLOCK, seq)
    n_far = 2
    assert (n_far - 1) * blk + 1 > MAX_DISTANCE or seq // blk <= n_far
    ii = jnp.arange(blk)[:, None]
    jj = jnp.arange(blk)[None, :]
    tiles = jnp.concatenate([t5_bucket(d * blk + ii - jj) for d in range(n_far + 1)], axis=0)
    bias_p = bias_lookup(rel_bias, tiles).reshape(a_heads, n_far + 1, blk, blk)
    q_pos_s = past + jnp.arange(dec_seq)
    k_pos_s = jnp.arange(past + LANE)
    rel_s = jnp.pad(q_pos_s[:, None] - k_pos_s[None, :], ((0, 8 - dec_seq), (0, 0)))
    bias_s = bias_lookup(rel_bias, t5_bucket(rel_s))[:, :dec_seq]
    bias_s = bias_s.reshape(A_KV_HEADS, 1, a_group, dec_seq, -1)
    bias_s = jnp.broadcast_to(bias_s, (A_KV_HEADS, 2, a_group, dec_seq, bias_s.shape[-1]))
    bias_s = bias_s.reshape(A_KV_HEADS * 2 * a_group * dec_seq, -1)

    cos_p, sin_p = _rope_tables(jnp.arange(seq), batch)
    cos_s, sin_s = _rope_tables(q_pos_s, dec_batch)

    ca_k = cache_a_k.reshape(n_even, n_pool, page * A_KV_HEADS, HEAD_DIM)
    ca_v = cache_a_v.reshape(n_even, n_pool, page * A_KV_HEADS, HEAD_DIM)
    cc_k = cache_c_k.reshape(cache_c_k.shape[0], n_pool, page * C_KV_HEADS, HEAD_DIM)
    cc_v = cache_c_v.reshape(cache_c_v.shape[0], n_pool, page * C_KV_HEADS, HEAD_DIM)
    cb_rope_t = jnp.swapaxes(cache_b_rope, 2, 3)
    cc_logf_t = jnp.swapaxes(cache_c_logf, 2, 3)

    def trunk(x, sample):
        nb, t = (dec_batch, dec_seq) if sample else (batch, seq)
        m = nb * t
        x = x.reshape(m, d_model)
        cos, sin = (cos_s, sin_s) if sample else (cos_p, sin_p)
        rows_even, rows_odd = [], []
        for layer in range(depth):
            li = layer // 2
            if layer % 2 == 0:
                lam_init = 0.8 - 0.6 * math.exp(-0.3 * layer)
                lam = lam_terms[li, :1].reshape(1, 1)
                proj = norm_matmul(x, g_mix_pre[layer], w_in_e[li])
                k_a = proj[:, ecols["ak"]:ecols["av"]]
                v_a = proj[:, ecols["av"]:ecols["bq"]]
                qlat, qpe, c_kv, ckv_bf, kpe2, kpe_bf = mla_prep(
                    proj, b_q_norm_g[li], b_kv_norm_g[li], w_uq[li], w_ukT[li], cos, sin, ecols)
                k_pe = kpe2[:, :B_ROPE]
                if sample:
                    q_a = proj[:, :aq_cols].reshape(nb, t, A_KV_HEADS, a_group, 2, a_half)
                    proj3 = proj.reshape(nb, t, -1)
                    oa = attn_a_decode(page_table, lam, _decode_queries(q_a), proj3, ecols, bias_s, a_subln_g[li],
                                       ca_k, ca_v, li, lam_init)
                    a_out = jnp.transpose(oa.reshape(nb, A_KV_HEADS, a_group, t, HEAD_DIM),
                                          (0, 3, 1, 2, 4)).reshape(m, -1).astype(BF16)
                    qlat3 = jnp.transpose(qlat.reshape(nb, t, b_heads, kv_rank).astype(F32),
                                          (0, 2, 1, 3)).reshape(nb, b_heads * t, kv_rank)
                    qpe3 = jnp.transpose(qpe.reshape(nb, t, b_heads, B_ROPE).astype(F32),
                                         (0, 2, 1, 3)).reshape(nb, b_heads * t, B_ROPE)
                    ob = attn_b_decode(page_table, qlat3, qpe3, c_kv.reshape(nb, t, -1),
                                       kpe2.reshape(nb, t, -1), cache_b_latent, cb_rope_t, li, b_scale)
                    lat = jnp.transpose(ob.reshape(nb, b_heads, t, kv_rank), (0, 2, 1, 3)).reshape(m, -1)
                    b_out = head_matmul(lat, w_uv[li])
                else:
                    kv_bf = proj[:, ecols["ak"]:ecols["bq"]].astype(BF16)
                    a_out = attn_a_prompt(lam, proj, kv_bf, bias_p, a_subln_g[li], nb, a_heads,
                                          lam_init, ecols["aq"])
                    b_out = attn_b_prompt(qlat, qpe, ckv_bf, kpe_bf, w_uv[li], nb, b_scale)
                h = jnp.concatenate([a_out, b_out], axis=1)
                x = matmul_norm_residual(h, w_o_e[li], g_mix_post[layer], x)
                rows_even.append((k_a.reshape(nb, t, A_KV_HEADS, HEAD_DIM),
                                  v_a.reshape(nb, t, A_KV_HEADS, HEAD_DIM),
                                  c_kv.reshape(nb, t, kv_rank), k_pe.reshape(nb, t, B_ROPE)))
            else:
                proj = norm_matmul(x, g_mix_pre[layer], w_in_o[li])
                k_c = proj[:, ocols["k"]:ocols["v"]]
                v_c = proj[:, ocols["v"]:ocols["f"]]
                logf_pad, c_pad = logf_cumsum(proj, ocols["f"], c_forget_b[li], nb, cumulate=not sample)
                logf = logf_pad[:, :c_heads]
                if sample:
                    q_c = jnp.transpose(proj[:, :cq_cols].reshape(nb, t, C_KV_HEADS, c_group, HEAD_DIM),
                                        (0, 2, 1, 3, 4)).reshape(nb, C_KV_HEADS, t * c_group, HEAD_DIM)
                    oc = attn_c_decode(page_table, q_c, proj.reshape(nb, t, -1),
                                       logf_pad.reshape(nb, t, LANE), ocols, cc_k, cc_v, cc_logf_t, li)
                    h = jnp.transpose(oc.reshape(nb, C_KV_HEADS, t, c_group, HEAD_DIM),
                                      (0, 2, 1, 3, 4)).reshape(m, -1).astype(BF16)
                else:
                    kv_bf = proj[:, ocols["k"]:ocols["f"]].astype(BF16)
                    c_t = jnp.transpose(c_pad[:, :c_heads].reshape(nb, t, c_heads), (0, 2, 1))
                    h = attn_c_prompt(proj, kv_bf, c_t, nb, c_heads)
                x = matmul_norm_residual(h, w_o_o[li], g_mix_post[layer], x)
                rows_odd.append((k_c.reshape(nb, t, C_KV_HEADS, HEAD_DIM),
                                 v_c.reshape(nb, t, C_KV_HEADS, HEAD_DIM),
                                 logf.reshape(nb, t, c_heads)))
            u = norm_matmul(x, g_ffn_pre[layer], w_up[layer], act=True, out_dtype=BF16)
            x = matmul_norm_residual(u, w_down[layer], g_ffn_post[layer], x)
        even = [jnp.stack(r) for r in zip(*rows_even)]
        odd = [jnp.stack(r) for r in zip(*rows_odd)]
        return x.reshape(nb, t, d_model), even, odd

    y_prompt, (pa_k, pa_v, pb_lat, pb_rope), (pc_k, pc_v, pc_logf) = trunk(x_prompt, False)
    y_sample, (sa_k, sa_v, sb_lat, sb_rope), (sc_k, sc_v, sc_logf) = trunk(x_sample, True)
    return (y_prompt, y_sample, pa_k, sa_k, pa_v, sa_v, pb_lat, sb_lat, pb_rope, sb_rope,
            pc_k, sc_k, pc_v, sc_v, pc_logf, sc_logf)
```

```python
import functools
import math

import jax
import jax.numpy as jnp
from jax import lax
from jax.experimental import pallas as pl
from jax.experimental.pallas import tpu as pltpu

F32 = jnp.float32
BF16 = jnp.bfloat16

HEAD_DIM = 128
A_KV_HEADS = 2
C_KV_HEADS = 2
B_ROPE = 64
N_BUCKETS = 32
MAX_DISTANCE = 128
ROPE_THETA = 10000.0
EPS = 1e-6
A_SCALE = (HEAD_DIM // 2) ** -0.5
assert math.frexp(A_SCALE)[0] == 0.5
NEG = -1e30
LANE = 128
VMEM_LIMIT = 56 * 1024 * 1024

ATT_BLOCK = 512
PAGES_PER_STEP = 32


def _params(*sem):
    return pltpu.CompilerParams(dimension_semantics=sem, vmem_limit_bytes=VMEM_LIMIT)


def _nt_dot(a, b):
    return lax.dot_general(a, b, (((1,), (1,)), ((), ())), preferred_element_type=F32)


def _rms(x, g):
    var = jnp.mean(x * x, axis=-1, keepdims=True)
    return x * lax.rsqrt(var + EPS) * g


def _norm_matmul_kernel(x_ref, g_ref, w_ref, o_ref, hn_ref, *, act):
    @pl.when(pl.program_id(1) == 0)
    def _():
        hn_ref[...] = _rms(x_ref[...], g_ref[...]).astype(BF16)

    y = jnp.dot(hn_ref[...], w_ref[...], preferred_element_type=F32)
    if act:
        y = jnp.square(jnp.maximum(y, 0.0))
    o_ref[...] = y.astype(o_ref.dtype)


def norm_matmul(x, g, w, *, act=False, out_dtype=F32):
    m, d = x.shape
    n = w.shape[1]
    tm = min(m, 1024 if out_dtype == BF16 else 512)
    tn = max(t for t in range(2 * LANE, 10 * LANE + 1, 2 * LANE) if n % t == 0)
    return pl.pallas_call(
        functools.partial(_norm_matmul_kernel, act=act),
        grid=(m // tm, n // tn),
        in_specs=[
            pl.BlockSpec((tm, d), lambda i, j: (i, 0)),
            pl.BlockSpec((1, d), lambda i, j: (0, 0)),
            pl.BlockSpec((d, tn), lambda i, j: (0, j)),
        ],
        out_specs=pl.BlockSpec((tm, tn), lambda i, j: (i, j)),
        out_shape=jax.ShapeDtypeStruct((m, n), out_dtype),
        scratch_shapes=[pltpu.VMEM((tm, d), BF16)],
        compiler_params=_params("parallel", "arbitrary"),
        name="norm_matmul",
    )(x, g.reshape(1, d), w)


def _matmul_norm_res_kernel(h_ref, w_ref, g_ref, x_ref, o_ref, acc_ref):
    k = pl.program_id(1)

    @pl.when(k == 0)
    def _():
        acc_ref[...] = jnp.zeros_like(acc_ref)

    acc_ref[...] += jnp.dot(h_ref[...], w_ref[...], preferred_element_type=F32)

    @pl.when(k == pl.num_programs(1) - 1)
    def _():
        o_ref[...] = x_ref[...] + _rms(acc_ref[...], g_ref[...])


def matmul_norm_residual(h, w, g, x):
    m, kd = h.shape
    d = w.shape[1]
    tm = min(m, 512)
    tk = min(kd, 2048)
    return pl.pallas_call(
        _matmul_norm_res_kernel,
        grid=(m // tm, kd // tk),
        in_specs=[
            pl.BlockSpec((tm, tk), lambda i, k: (i, k)),
            pl.BlockSpec((tk, d), lambda i, k: (k, 0)),
            pl.BlockSpec((1, d), lambda i, k: (0, 0)),
            pl.BlockSpec((tm, d), lambda i, k: (i, 0)),
        ],
        out_specs=pl.BlockSpec((tm, d), lambda i, k: (i, 0)),
        out_shape=jax.ShapeDtypeStruct((m, d), F32),
        scratch_shapes=[pltpu.VMEM((tm, d), F32)],
        compiler_params=_params("parallel", "arbitrary"),
        name="matmul_norm_residual",
    )(h, w, g.reshape(1, d), x)


def _head_matmul_kernel(x_ref, w_ref, o_ref):
    o_ref[...] = jnp.dot(x_ref[...].astype(BF16), w_ref[...],
                         preferred_element_type=F32).astype(o_ref.dtype)


def head_matmul(x, w, *, out_dtype=BF16):
    m = x.shape[0]
    nh_heads, kh, nh = w.shape
    tm = min(m, 512)
    return pl.pallas_call(
        _head_matmul_kernel,
        grid=(m // tm, nh_heads),
        in_specs=[
            pl.BlockSpec((tm, kh), lambda i, h: (i, h)),
            pl.BlockSpec((None, kh, nh), lambda i, h: (h, 0, 0)),
        ],
        out_specs=pl.BlockSpec((tm, nh), lambda i, h: (i, h)),
        out_shape=jax.ShapeDtypeStruct((m, nh_heads * nh), out_dtype),
        compiler_params=_params("parallel", "arbitrary"),
        name="head_matmul",
    )(x, w)


def _rope128(x, cos, sin):
    lane = lax.broadcasted_iota(jnp.int32, x.shape, 1)
    low = (lane % B_ROPE) < (B_ROPE // 2)
    rot = jnp.where(low, -pltpu.roll(x, LANE - B_ROPE // 2, 1), pltpu.roll(x, B_ROPE // 2, 1))
    return x * cos + rot * sin


def _mla_prep_kernel(bq_ref, ckv_ref, kpe_ref, gq_ref, gkv_ref, wuq_ref, wuk_ref, cos_ref, sin_ref,
                     qlat_ref, qpe_ref, ckv_out_ref, ckv_bf_ref, kpe_out_ref, kpe_bf_ref, *, n_heads):
    cos = cos_ref[...]
    sin = sin_ref[...]
    c_q = _rms(bq_ref[...], gq_ref[...]).astype(BF16)
    qb = jnp.dot(c_q, wuq_ref[...], preferred_element_type=F32)
    nope = n_heads * HEAD_DIM
    for h in range(n_heads):
        qn = qb[:, h * HEAD_DIM:(h + 1) * HEAD_DIM].astype(BF16)
        qlat_ref[:, h * 256:(h + 1) * 256] = jnp.dot(
            qn, wuk_ref[h], preferred_element_type=F32).astype(BF16)
    for c in range(n_heads * B_ROPE // LANE):
        x = qb[:, nope + c * LANE: nope + (c + 1) * LANE]
        qpe_ref[:, c * LANE:(c + 1) * LANE] = _rope128(x, cos, sin).astype(BF16)
    c_kv = _rms(ckv_ref[...], gkv_ref[...])
    ckv_out_ref[...] = c_kv
    ckv_bf_ref[...] = c_kv.astype(BF16)
    kr = _rope128(kpe_ref[...], cos, sin)
    lane = lax.broadcasted_iota(jnp.int32, kr.shape, 1)
    kr = jnp.where(lane < B_ROPE, kr, 0.0)
    kd = kr + pltpu.roll(kr, B_ROPE, 1)
    kpe_out_ref[...] = kd
    kpe_bf_ref[...] = kd.astype(BF16)


def mla_prep(proj, gq, gkv, wuq, wukT, cos, sin, cols):
    m = proj.shape[0]
    n_heads, _, kv_rank = wukT.shape
    q_rank = wuq.shape[0]
    tm = min(m, 512)
    assert cols["bq"] % q_rank == 0 and cols["ckv"] % kv_rank == 0 and cols["kpe"] % LANE == 0
    outs = (
        jax.ShapeDtypeStruct((m, n_heads * kv_rank), BF16),
        jax.ShapeDtypeStruct((m, n_heads * B_ROPE), BF16),
        jax.ShapeDtypeStruct((m, kv_rank), F32),
        jax.ShapeDtypeStruct((m, kv_rank), BF16),
        jax.ShapeDtypeStruct((m, LANE), F32),
        jax.ShapeDtypeStruct((m, LANE), BF16),
    )
    row = lambda w: pl.BlockSpec((tm, w), lambda i: (i, 0))
    return pl.pallas_call(
        functools.partial(_mla_prep_kernel, n_heads=n_heads),
        grid=(m // tm,),
        in_specs=[
            pl.BlockSpec((tm, q_rank), lambda i: (i, cols["bq"] // q_rank)),
            pl.BlockSpec((tm, kv_rank), lambda i: (i, cols["ckv"] // kv_rank)),
            pl.BlockSpec((tm, LANE), lambda i: (i, cols["kpe"] // LANE)),
            pl.BlockSpec((1, q_rank), lambda i: (0, 0)),
            pl.BlockSpec((1, kv_rank), lambda i: (0, 0)),
            pl.BlockSpec(wuq.shape, lambda i: (0, 0)),
            pl.BlockSpec(wukT.shape, lambda i: (0, 0, 0)),
            row(LANE), row(LANE),
        ],
        out_specs=[row(n_heads * kv_rank), row(n_heads * B_ROPE), row(kv_rank), row(kv_rank),
                   row(LANE), row(LANE)],
        out_shape=outs,
        compiler_params=_params("parallel"),
        name="mla_prep",
    )(proj, proj, proj, gq.reshape(1, -1), gkv.reshape(1, -1), wuq, wukT, cos, sin)


def _lambda_kernel(a_ref, b_ref, c_ref, d_ref, o_ref):
    s1 = jnp.sum(a_ref[...] * b_ref[...], axis=-1, keepdims=True)
    s2 = jnp.sum(c_ref[...] * d_ref[...], axis=-1, keepdims=True)
    o_ref[...] = jnp.broadcast_to(jnp.exp(s1) - jnp.exp(s2), o_ref.shape)


def lambda_terms(a_lambda):
    n = a_lambda.shape[0]
    parts = [a_lambda[:, i, :] for i in range(4)]
    return pl.pallas_call(
        _lambda_kernel,
        out_shape=jax.ShapeDtypeStruct((n, LANE), F32),
        name="lambda_terms",
    )(*parts)


def _bias_lookup_kernel(tab_ref, bucket_ref, o_ref):
    h = pl.program_id(0)
    bucket = bucket_ref[...]
    out = jnp.full(bucket.shape, tab_ref[h, N_BUCKETS - 1], F32)
    for b in range(N_BUCKETS - 1):
        out = jnp.where(bucket == b, tab_ref[h, b], out)
    o_ref[...] = out


def bias_lookup(rel_bias, bucket):
    n_heads = rel_bias.shape[1]
    r, s = bucket.shape
    return pl.pallas_call(
        _bias_lookup_kernel,
        grid=(n_heads,),
        in_specs=[
            pl.BlockSpec(memory_space=pltpu.SMEM),
            pl.BlockSpec((r, s), lambda h: (0, 0)),
        ],
        out_specs=pl.BlockSpec((None, r, s), lambda h: (h, 0, 0)),
        out_shape=jax.ShapeDtypeStruct((n_heads, r, s), F32),
        compiler_params=_params("arbitrary"),
        name="bias_lookup",
    )(rel_bias.T, bucket)


def t5_bucket(rel):
    n = jnp.maximum(rel, 0)
    max_exact = N_BUCKETS // 2
    nf = jnp.maximum(n, 1).astype(F32)
    large = max_exact + (jnp.log(nf / max_exact) / math.log(MAX_DISTANCE / max_exact)
                         * (N_BUCKETS - max_exact)).astype(jnp.int32)
    return jnp.where(n < max_exact, n, jnp.minimum(large, N_BUCKETS - 1))


def _split3(x):
    x1 = x.astype(BF16)
    r1 = x - x1.astype(F32)
    x2 = r1.astype(BF16)
    r2 = r1 - x2.astype(F32)
    return x1, x2, r2.astype(BF16)


def _logf_cumsum_kernel(f_ref, b_ref, logf_ref, c_ref, carry_ref, *, cumulate):
    x = f_ref[...] + b_ref[...]
    logf = jnp.minimum(x, 0.0) - jnp.log1p(jnp.exp(-jnp.abs(x)))
    logf_ref[...] = logf
    if cumulate:
        @pl.when(pl.program_id(1) == 0)
        def _():
            carry_ref[...] = jnp.zeros_like(carry_ref)

        t = logf.shape[0]
        tri = (lax.broadcasted_iota(jnp.int32, (t, t), 0)
               >= lax.broadcasted_iota(jnp.int32, (t, t), 1)).astype(BF16)
        c = carry_ref[...]
        for piece in _split3(logf):
            c = c + jnp.dot(tri, piece, preferred_element_type=F32)
        c_ref[...] = c
        carry_ref[...] = c[t - 1:t, :]
    else:
        c_ref[...] = logf


def logf_cumsum(proj, col, b_f, batch, *, cumulate):
    m = proj.shape[0]
    t = m // batch
    tt = min(t, 256) if cumulate else min(m, 512)
    nb = t // tt if cumulate else m // tt
    grid = (batch, nb) if cumulate else (1, nb)
    bpad = jnp.zeros((1, LANE), F32).at[0, :b_f.shape[0]].set(b_f)
    spec = pl.BlockSpec((tt, LANE), lambda b, i: (b * nb + i, 0))
    return pl.pallas_call(
        functools.partial(_logf_cumsum_kernel, cumulate=cumulate),
        grid=grid,
        in_specs=[
            pl.BlockSpec((tt, LANE), lambda b, i: (b * nb + i, col // LANE)),
            pl.BlockSpec((1, LANE), lambda b, i: (0, 0)),
        ],
        out_specs=[spec, spec],
        out_shape=(jax.ShapeDtypeStruct((m, LANE), F32), jax.ShapeDtypeStruct((m, LANE), F32)),
        scratch_shapes=[pltpu.VMEM((1, LANE), F32)],
        compiler_params=_params("arbitrary", "arbitrary"),
        name="logf_cumsum",
    )(proj, bpad)


def _online_update(state, s, v):
    m, l, acc = state
    m_new = jnp.maximum(m, jnp.max(s, axis=-1, keepdims=True))
    alpha = jnp.exp(m - m_new)
    p = jnp.exp(s - m_new)
    l = alpha * l + jnp.sum(p, axis=-1, keepdims=True)
    acc = alpha * acc + jnp.dot(p.astype(BF16), v, preferred_element_type=F32)
    return m_new, l, acc


def _init_state(rows, width):
    return (jnp.full((rows, 1), NEG, F32), jnp.zeros((rows, 1), F32), jnp.zeros((rows, width), F32))


def _causal_sweep(step, init, qi):
    state = lax.fori_loop(0, qi, lambda j, st: step(j, st, False), init)
    return step(qi, state, True)


def _local_causal(blk):
    return (lax.broadcasted_iota(jnp.int32, (blk, blk), 1)
            <= lax.broadcasted_iota(jnp.int32, (blk, blk), 0))


def _attn_a_kernel(lam_ref, q_ref, k_ref, v_ref, bias_ref, g_ref, o_ref, *, blk, lam_init, n_far):
    qi = pl.program_id(2)
    q = q_ref[...] * A_SCALE
    lane = lax.broadcasted_iota(jnp.int32, q.shape, 1)
    qc = (jnp.where(lane < HEAD_DIM // 2, q, 0.0).astype(BF16),
          jnp.where(lane >= HEAD_DIM // 2, q, 0.0).astype(BF16))
    causal = _local_causal(blk)

    def step(j, state, masked):
        start = pl.multiple_of(j * blk, blk)
        k = k_ref[pl.ds(start, blk), :]
        v = v_ref[pl.ds(start, blk), :]
        bias = bias_ref[jnp.minimum(qi - j, n_far)]
        out = []
        for c in range(2):
            s = _nt_dot(qc[c], k) + bias
            if masked:
                s = jnp.where(causal, s, NEG)
            out.append(_online_update(state[c], s, v))
        return tuple(out)

    init = (_init_state(blk, HEAD_DIM), _init_state(blk, HEAD_DIM))
    (m0, l0, a0), (m1, l1, a1) = _causal_sweep(step, init, qi)
    lam = lam_ref[0, 0] + lam_init
    o = a0 / l0 - lam * (a1 / l1)
    o_ref[...] = (_rms(o, g_ref[...]) * (1.0 - lam_init)).astype(o_ref.dtype)


def attn_a_prompt(lam, proj, kv_bf, bias, subln_g, batch, n_heads, lam_init, qcol):
    m = proj.shape[0]
    t = m // batch
    blk = min(ATT_BLOCK, t)
    nq = t // blk
    group = n_heads // A_KV_HEADS
    n_far = bias.shape[1] - 1
    return pl.pallas_call(
        functools.partial(_attn_a_kernel, blk=blk, lam_init=lam_init, n_far=n_far),
        grid=(batch, n_heads, nq),
        in_specs=[
            pl.BlockSpec(memory_space=pltpu.SMEM),
            pl.BlockSpec((blk, HEAD_DIM), lambda b, h, i: (b * nq + i, qcol // HEAD_DIM + h)),
            pl.BlockSpec((t, HEAD_DIM), lambda b, h, i: (b, h // group)),
            pl.BlockSpec((t, HEAD_DIM), lambda b, h, i: (b, A_KV_HEADS + h // group)),
            pl.BlockSpec((None,) + bias.shape[1:], lambda b, h, i: (h, 0, 0, 0)),
            pl.BlockSpec((1, HEAD_DIM), lambda b, h, i: (0, 0)),
        ],
        out_specs=pl.BlockSpec((blk, HEAD_DIM), lambda b, h, i: (b * nq + i, h)),
        out_shape=jax.ShapeDtypeStruct((m, n_heads * HEAD_DIM), BF16),
        compiler_params=_params("parallel", "arbitrary", "arbitrary"),
        name="attn_a_prompt",
    )(lam, proj, kv_bf, kv_bf, bias, subln_g.reshape(1, -1))


def _attn_b_kernel(qlat_ref, qpe_ref, ckv_ref, kpe_ref, wuv_ref, o_ref, *, blk, scale, hp):
    qi = pl.program_id(2)
    kv_rank = ckv_ref.shape[1]
    dv = wuv_ref.shape[2]
    qpe = qpe_ref[...]
    lane = lax.broadcasted_iota(jnp.int32, qpe.shape, 1)
    qlat = [qlat_ref[:, u * kv_rank:(u + 1) * kv_rank] for u in range(hp)]
    qrope = [jnp.where((lane // B_ROPE) == u, qpe, jnp.zeros_like(qpe)) for u in range(hp)]
    causal = _local_causal(blk)

    def step(j, state, masked):
        start = pl.multiple_of(j * blk, blk)
        ckv = ckv_ref[pl.ds(start, blk), :]
        kpe = kpe_ref[pl.ds(start, blk), :]
        out = []
        for u in range(hp):
            s = (_nt_dot(qlat[u], ckv) + _nt_dot(qrope[u], kpe)) * scale
            if masked:
                s = jnp.where(causal, s, NEG)
            out.append(_online_update(state[u], s, ckv))
        return tuple(out)

    final = _causal_sweep(step, tuple(_init_state(blk, kv_rank) for _ in range(hp)), qi)
    for u, (m, l, acc) in enumerate(final):
        lat = (acc / l).astype(BF16)
        o_ref[:, u * dv:(u + 1) * dv] = jnp.dot(lat, wuv_ref[u],
                                                preferred_element_type=F32).astype(o_ref.dtype)


def attn_b_prompt(qlat, qpe, ckv_bf, kpe_bf, wuv, batch, scale):
    m = qlat.shape[0]
    n_heads, kv_rank, dv = wuv.shape
    t = m // batch
    blk = min(ATT_BLOCK, t)
    nq = t // blk
    hp = LANE // B_ROPE
    return pl.pallas_call(
        functools.partial(_attn_b_kernel, blk=blk, scale=scale, hp=hp),
        grid=(batch, n_heads // hp, nq),
        in_specs=[
            pl.BlockSpec((blk, hp * kv_rank), lambda b, h, i: (b * nq + i, h)),
            pl.BlockSpec((blk, LANE), lambda b, h, i: (b * nq + i, h)),
            pl.BlockSpec((t, kv_rank), lambda b, h, i: (b, 0)),
            pl.BlockSpec((t, LANE), lambda b, h, i: (b, 0)),
            pl.BlockSpec((hp, kv_rank, dv), lambda b, h, i: (h, 0, 0)),
        ],
        out_specs=pl.BlockSpec((blk, hp * dv), lambda b, h, i: (b * nq + i, h)),
        out_shape=jax.ShapeDtypeStruct((m, n_heads * dv), BF16),
        compiler_params=_params("parallel", "arbitrary", "arbitrary"),
        name="attn_b_prompt",
    )(qlat, qpe, ckv_bf, kpe_bf, wuv)


def _attn_c_kernel(q_ref, k_ref, v_ref, c_ref, o_ref, *, blk, hp):
    hq = pl.program_id(1)
    qi = pl.program_id(2)
    scale = HEAD_DIM ** -0.5
    causal = _local_causal(blk)
    qstart = pl.multiple_of(qi * blk, blk)
    eye = (lax.broadcasted_iota(jnp.int32, (blk, blk), 0)
           == lax.broadcasted_iota(jnp.int32, (blk, blk), 1))
    q, cq = [], []
    for u in range(hp):
        q.append((q_ref[:, u * HEAD_DIM:(u + 1) * HEAD_DIM] * scale).astype(BF16))
        cq_row = c_ref[pl.ds(hq * hp + u, 1), pl.ds(qstart, blk)]
        cq.append(jnp.sum(jnp.where(eye, cq_row, 0.0), axis=-1, keepdims=True))

    def step(j, state, masked):
        start = pl.multiple_of(j * blk, blk)
        k = k_ref[pl.ds(start, blk), :]
        v = v_ref[pl.ds(start, blk), :]
        out = []
        for u in range(hp):
            ck = c_ref[pl.ds(hq * hp + u, 1), pl.ds(start, blk)]
            s = _nt_dot(q[u], k) + (cq[u] - ck)
            if masked:
                s = jnp.where(causal, s, NEG)
            out.append(_online_update(state[u], s, v))
        return tuple(out)

    final = _causal_sweep(step, tuple(_init_state(blk, HEAD_DIM) for _ in range(hp)), qi)
    for u, (m, l, acc) in enumerate(final):
        o_ref[:, u * HEAD_DIM:(u + 1) * HEAD_DIM] = (acc / l).astype(o_ref.dtype)


def attn_c_prompt(proj, kv_bf, c_t, batch, n_heads, hp=2):
    m = proj.shape[0]
    t = m // batch
    blk = min(ATT_BLOCK, t)
    nq = t // blk
    group = n_heads // C_KV_HEADS
    assert group % hp == 0
    return pl.pallas_call(
        functools.partial(_attn_c_kernel, blk=blk, hp=hp),
        grid=(batch, n_heads // hp, nq),
        in_specs=[
            pl.BlockSpec((blk, hp * HEAD_DIM), lambda b, h, i: (b * nq + i, h)),
            pl.BlockSpec((t, HEAD_DIM), lambda b, h, i: (b, h * hp // group)),
            pl.BlockSpec((t, HEAD_DIM), lambda b, h, i: (b, C_KV_HEADS + h * hp // group)),
            pl.BlockSpec((None, n_heads, t), lambda b, h, i: (b, 0, 0)),
        ],
        out_specs=pl.BlockSpec((blk, hp * HEAD_DIM), lambda b, h, i: (b * nq + i, h)),
        out_shape=jax.ShapeDtypeStruct((m, n_heads * HEAD_DIM), BF16),
        compiler_params=_params("parallel", "arbitrary", "arbitrary"),
        name="attn_c_prompt",
    )(proj, kv_bf, kv_bf, c_t)


def _scratch_update(m_ref, l_ref, acc_ref, s, v_list):
    m = m_ref[...]
    m_new = jnp.maximum(m, jnp.max(s, axis=-1, keepdims=True))
    alpha = jnp.exp(m - m_new)
    p = jnp.exp(s - m_new)
    l_ref[...] = alpha * l_ref[...] + jnp.sum(p, axis=-1, keepdims=True)
    pb = p.astype(BF16)
    per = s.shape[0] // len(v_list)
    pv = [jnp.dot(pb[i * per:(i + 1) * per], v, preferred_element_type=F32) for i, v in enumerate(v_list)]
    acc_ref[...] = alpha * acc_ref[...] + (pv[0] if len(pv) == 1 else jnp.concatenate(pv, axis=0))
    m_ref[...] = m_new


def _head_rows(ref, kh, n_keys, n_kv):
    return ref[pl.ds(kh, n_keys, stride=n_kv), :]


def _rows_dst(buf, i):
    n = buf.shape[0] // PAGES_PER_STEP
    return buf.at[pl.ds(i * n, n)]


def _lanes_dst(buf, i):
    n = buf.shape[1] // PAGES_PER_STEP
    return buf.at[:, pl.ds(i * n, n)]


def _page_copies(pt_ref, streams, sem, seq, grp, slot, *, li, n_pages, reverse):
    out = []
    for i in range(PAGES_PER_STEP):
        logical = grp * PAGES_PER_STEP + i
        if reverse:
            logical = n_pages - 1 - logical
        phys = pt_ref[seq * n_pages + logical]
        for a, (hbm, buf, dst) in enumerate(streams):
            out.append(pltpu.make_async_copy(hbm.at[li, phys], dst(buf.at[slot], i), sem.at[a, slot]))
    return out


def _fetch_pages(pt_ref, streams, sem, **kw):
    b, g = pl.program_id(0), pl.program_id(1)
    nb, ng = pl.num_programs(0), pl.num_programs(1)
    step = b * ng + g
    slot = step % 2
    copies = functools.partial(_page_copies, pt_ref, streams, sem, **kw)

    @pl.when(step == 0)
    def _():
        for c in copies(b, g, slot):
            c.start()

    @pl.when(step + 1 < nb * ng)
    def _():
        nxt = step + 1
        for c in copies(nxt // ng, nxt % ng, 1 - slot):
            c.start()

    for c in copies(b, g, slot):
        c.wait()
    return slot


def _stack(parts):
    return parts[0] if len(parts) == 1 else jnp.concatenate(parts, axis=0)


def _scratch_update_row(m_ref, l_ref, acc_ref, s_col, v_row):
    m = m_ref[...]
    m_new = jnp.maximum(m, s_col)
    alpha = jnp.exp(m - m_new)
    p = jnp.exp(s_col - m_new)
    l_ref[...] = alpha * l_ref[...] + p
    acc_ref[...] = alpha * acc_ref[...] + p * v_row
    m_ref[...] = m_new


def _init_scratch(m_ref, l_ref, acc_ref):
    m_ref[...] = jnp.full(m_ref.shape, NEG, F32)
    l_ref[...] = jnp.zeros(l_ref.shape, F32)
    acc_ref[...] = jnp.zeros(acc_ref.shape, F32)


def _dec_a_kernel(pt_ref, lam_ref, q_ref, kn_ref, vn_ref, bias_ref, g_ref, k_hbm, v_hbm,
                  o_ref, kbuf, vbuf, sem, m_ref, l_ref, acc_ref, *, li, page, n_pages, n_new, lam_init):
    g = pl.program_id(1)
    scale = (HEAD_DIM // 2) ** -0.5
    q = q_ref[...]
    n_kv, per, _ = q.shape
    rows = n_kv * per
    nk = PAGES_PER_STEP * page
    past = n_pages * page
    slot = _fetch_pages(pt_ref, [(k_hbm, kbuf, _rows_dst), (v_hbm, vbuf, _rows_dst)], sem,
                        li=li, n_pages=n_pages, reverse=False)

    @pl.when(g == 0)
    def _():
        _init_scratch(m_ref, l_ref, acc_ref)

    q_bf = q.astype(BF16)
    kb, vb = kbuf.at[slot], vbuf.at[slot]
    start = pl.multiple_of(g * nk, nk)
    s = _stack([_nt_dot(q_bf[kh], _head_rows(kb, kh, nk, n_kv).astype(BF16)) for kh in range(n_kv)])
    s = s * scale + bias_ref[:, pl.ds(start, nk)]
    _scratch_update(m_ref, l_ref, acc_ref, s,
                    [_head_rows(vb, kh, nk, n_kv).astype(BF16) for kh in range(n_kv)])

    @pl.when(g == pl.num_programs(1) - 1)
    def _():
        t_row = lax.broadcasted_iota(jnp.int32, (rows, 1), 0) % n_new
        kn = kn_ref[...]
        vn = vn_ref[...]
        for j in range(n_new):
            s = _stack([jnp.sum(q[kh] * kn[j:j + 1, kh * HEAD_DIM:(kh + 1) * HEAD_DIM],
                                axis=-1, keepdims=True) for kh in range(n_kv)]) * scale
            s = s + bias_ref[:, past + j:past + j + 1]
            s = jnp.where(t_row >= j, s, NEG)
            v_rows = _stack([jnp.broadcast_to(vn[j:j + 1, kh * HEAD_DIM:(kh + 1) * HEAD_DIM],
                                              (per, HEAD_DIM)) for kh in range(n_kv)])
            _scratch_update_row(m_ref, l_ref, acc_ref, s, v_rows)
        o = acc_ref[...] / l_ref[...]
        lam = lam_ref[0, 0] + lam_init
        half = per // 2
        for kh in range(n_kv):
            base = kh * per
            d = o[base:base + half] - lam * o[base + half:base + per]
            o_ref[kh] = _rms(d, g_ref[...]) * (1.0 - lam_init)


def attn_a_decode(page_table, lam, q4, proj3, cols, bias, subln_g, cache_k, cache_v, li, lam_init):
    nb, n_pages = page_table.shape
    n_kv, per = q4.shape[1], q4.shape[2]
    page = cache_k.shape[2] // n_kv
    width = n_kv * HEAD_DIM
    rows = n_kv * per
    n_new = proj3.shape[1]
    pg = PAGES_PER_STEP
    assert n_pages % pg == 0
    ng = n_pages // pg
    page_buf = pltpu.VMEM((2, pg * page * n_kv, HEAD_DIM), F32)

    in_specs = [
        pl.BlockSpec(memory_space=pltpu.SMEM),
        pl.BlockSpec((None, n_kv, per, HEAD_DIM), lambda b, g, pt: (b, 0, 0, 0)),
        pl.BlockSpec((None, n_new, width), lambda b, g, pt: (b, 0, cols["ak"] // width)),
        pl.BlockSpec((None, n_new, width), lambda b, g, pt: (b, 0, cols["av"] // width)),
        pl.BlockSpec(bias.shape, lambda b, g, pt: (0, 0)),
        pl.BlockSpec((1, HEAD_DIM), lambda b, g, pt: (0, 0)),
        pl.BlockSpec(memory_space=pl.ANY),
        pl.BlockSpec(memory_space=pl.ANY),
    ]
    return pl.pallas_call(
        functools.partial(_dec_a_kernel, li=li, page=page, n_pages=n_pages, n_new=n_new,
                          lam_init=lam_init),
        grid_spec=pltpu.PrefetchScalarGridSpec(
            num_scalar_prefetch=1,
            grid=(nb, ng),
            in_specs=in_specs,
            out_specs=pl.BlockSpec((None, n_kv, per // 2, HEAD_DIM), lambda b, g, pt: (b, 0, 0, 0)),
            scratch_shapes=[page_buf, page_buf, pltpu.SemaphoreType.DMA((2, 2)),
                            pltpu.VMEM((rows, 1), F32), pltpu.VMEM((rows, 1), F32),
                            pltpu.VMEM((rows, HEAD_DIM), F32)],
        ),
        out_shape=jax.ShapeDtypeStruct((nb, n_kv, per // 2, HEAD_DIM), F32),
        compiler_params=_params("arbitrary", "arbitrary"),
        name="attn_a_decode",
    )(page_table.reshape(-1), lam, q4, proj3, proj3, bias, subln_g.reshape(1, -1), cache_k, cache_v)


def _dec_b_kernel(pt_ref, qlat_ref, qpe_ref, cn_ref, kn_ref, lat_hbm, rope_hbm,
                  o_ref, latbuf, ropebuf, sem, m_ref, l_ref, acc_ref, *, li, n_pages, n_new, scale):
    g = pl.program_id(1)
    qlat = qlat_ref[...]
    qpe = qpe_ref[...]
    rows = qlat.shape[0]
    slot = _fetch_pages(pt_ref, [(lat_hbm, latbuf, _rows_dst), (rope_hbm, ropebuf, _lanes_dst)], sem,
                        li=li, n_pages=n_pages, reverse=False)

    @pl.when(g == 0)
    def _():
        _init_scratch(m_ref, l_ref, acc_ref)

    lat = latbuf[slot].astype(BF16)
    s = (_nt_dot(qlat.astype(BF16), lat)
         + jnp.dot(qpe.astype(BF16), ropebuf[slot].astype(BF16), preferred_element_type=F32)) * scale
    _scratch_update(m_ref, l_ref, acc_ref, s, [lat])

    @pl.when(g == pl.num_programs(1) - 1)
    def _():
        t_row = lax.broadcasted_iota(jnp.int32, (rows, 1), 0) % n_new
        cn = cn_ref[...]
        kn = kn_ref[...][:, :B_ROPE]
        for j in range(n_new):
            s = (jnp.sum(qlat * cn[j:j + 1, :], axis=-1, keepdims=True)
                 + jnp.sum(qpe * kn[j:j + 1, :], axis=-1, keepdims=True)) * scale
            s = jnp.where(t_row >= j, s, NEG)
            _scratch_update_row(m_ref, l_ref, acc_ref, s, cn[j:j + 1, :])
        o_ref[...] = acc_ref[...] / l_ref[...]


def attn_b_decode(page_table, qlat3, qpe3, ckv3, kpe3, cache_lat, cache_rope_t, li, scale):
    nb, n_pages = page_table.shape
    page = cache_lat.shape[2]
    kv_rank = cache_lat.shape[3]
    rows = qlat3.shape[1]
    n_new = ckv3.shape[1]
    pg = PAGES_PER_STEP
    assert n_pages % pg == 0
    ng = n_pages // pg

    in_specs = [
        pl.BlockSpec((None, rows, kv_rank), lambda b, g, pt: (b, 0, 0)),
        pl.BlockSpec((None, rows, B_ROPE), lambda b, g, pt: (b, 0, 0)),
        pl.BlockSpec((None, n_new, kv_rank), lambda b, g, pt: (b, 0, 0)),
        pl.BlockSpec((None, n_new, LANE), lambda b, g, pt: (b, 0, 0)),
        pl.BlockSpec(memory_space=pl.ANY),
        pl.BlockSpec(memory_space=pl.ANY),
    ]
    return pl.pallas_call(
        functools.partial(_dec_b_kernel, li=li, n_pages=n_pages, n_new=n_new, scale=scale),
        grid_spec=pltpu.PrefetchScalarGridSpec(
            num_scalar_prefetch=1,
            grid=(nb, ng),
            in_specs=in_specs,
            out_specs=pl.BlockSpec((None, rows, kv_rank), lambda b, g, pt: (b, 0, 0)),
            scratch_shapes=[pltpu.VMEM((2, pg * page, kv_rank), F32),
                            pltpu.VMEM((2, B_ROPE, pg * page), F32),
                            pltpu.SemaphoreType.DMA((2, 2)),
                            pltpu.VMEM((rows, 1), F32), pltpu.VMEM((rows, 1), F32),
                            pltpu.VMEM((rows, kv_rank), F32)],
        ),
        out_shape=jax.ShapeDtypeStruct((nb, rows, kv_rank), F32),
        compiler_params=_params("arbitrary", "arbitrary"),
        name="attn_b_decode",
    )(page_table.reshape(-1), qlat3, qpe3, ckv3, kpe3, cache_lat, cache_rope_t)


def _dec_c_kernel(pt_ref, q_ref, kn_ref, vn_ref, fn_ref, k_hbm, v_hbm, f_hbm,
                  o_ref, kbuf, vbuf, fbuf, sem, m_ref, l_ref, acc_ref, later_ref, newdecay_ref,
                  *, li, page, n_pages, n_new, n_heads):
    g = pl.program_id(1)
    scale = HEAD_DIM ** -0.5
    q = q_ref[...]
    n_kv, per, _ = q.shape
    rows = n_kv * per
    group = n_heads // n_kv
    nk = PAGES_PER_STEP * page
    row = lax.broadcasted_iota(jnp.int32, (rows, 1), 0)
    t_row = (row % per) // group
    head_row = (row // per) * group + row % group
    expand = head_row == lax.broadcasted_iota(jnp.int32, (rows, n_heads), 1)
    slot = _fetch_pages(pt_ref, [(k_hbm, kbuf, _rows_dst), (v_hbm, vbuf, _rows_dst),
                                 (f_hbm, fbuf, _rows_dst)], sem,
                        li=li, n_pages=n_pages, reverse=True)

    def new_logf_col(j):
        return jnp.sum(jnp.where(expand, fn_ref[...][j:j + 1, :n_heads], 0.0), axis=-1, keepdims=True)

    @pl.when(g == 0)
    def _():
        _init_scratch(m_ref, l_ref, acc_ref)
        kn = kn_ref[...]
        vn = vn_ref[...]
        cols = [new_logf_col(j) for j in range(n_new)]
        total = jnp.zeros((rows, 1), F32)
        for j in range(n_new):
            total = total + jnp.where(t_row >= j, cols[j], 0.0)
        newdecay_ref[...] = total
        later_ref[...] = jnp.zeros(later_ref.shape, F32)
        for j in range(n_new):
            decay = jnp.zeros((rows, 1), F32)
            for i in range(j + 1, n_new):
                decay = decay + jnp.where(t_row >= i, cols[i], 0.0)
            s = _stack([jnp.sum(q[kh] * kn[j:j + 1, kh * HEAD_DIM:(kh + 1) * HEAD_DIM],
                                axis=-1, keepdims=True) for kh in range(n_kv)]) * scale + decay
            s = jnp.where(t_row >= j, s, NEG)
            v_rows = _stack([jnp.broadcast_to(vn[j:j + 1, kh * HEAD_DIM:(kh + 1) * HEAD_DIM],
                                              (per, HEAD_DIM)) for kh in range(n_kv)])
            _scratch_update_row(m_ref, l_ref, acc_ref, s, v_rows)

    q_bf = q.astype(BF16)
    kb, vb = kbuf.at[slot], vbuf.at[slot]
    f = fbuf[slot]
    after = (lax.broadcasted_iota(jnp.int32, (page, page), 0)
             > lax.broadcasted_iota(jnp.int32, (page, page), 1)).astype(BF16)
    within = jnp.zeros(f.shape, F32)
    for piece in _split3(f):
        within = within + jnp.dot(piece, after, preferred_element_type=F32)
    page_sum = jnp.sum(f, axis=-1, keepdims=True)
    later = later_ref[...]
    per_page = []
    for i in range(PAGES_PER_STEP):
        sl = slice(i * n_heads, (i + 1) * n_heads)
        per_page.append(within[sl] + later)
        later = later + page_sum[sl]
    later_ref[...] = later
    by_head = jnp.concatenate(per_page, axis=1)
    decay = _stack([jnp.concatenate([by_head[kh * group:(kh + 1) * group]] * n_new, axis=0)
                    for kh in range(n_kv)])
    s = _stack([_nt_dot(q_bf[kh], _head_rows(kb, kh, nk, n_kv).astype(BF16)) for kh in range(n_kv)])
    s = s * scale + (decay + newdecay_ref[...])
    _scratch_update(m_ref, l_ref, acc_ref, s,
                    [_head_rows(vb, kh, nk, n_kv).astype(BF16) for kh in range(n_kv)])

    @pl.when(g == pl.num_programs(1) - 1)
    def _():
        o = acc_ref[...] / l_ref[...]
        for kh in range(n_kv):
            o_ref[kh] = o[kh * per:(kh + 1) * per]


def attn_c_decode(page_table, q4, proj3, logf3, cols, cache_k, cache_v, cache_f_t, li):
    nb, n_pages = page_table.shape
    n_kv, per = q4.shape[1], q4.shape[2]
    page = cache_k.shape[2] // n_kv
    width = n_kv * HEAD_DIM
    n_heads = cache_f_t.shape[2]
    rows = n_kv * per
    n_new = proj3.shape[1]
    pg = PAGES_PER_STEP
    assert n_pages % pg == 0
    ng = n_pages // pg
    page_buf = pltpu.VMEM((2, pg * page * n_kv, HEAD_DIM), F32)

    in_specs = [
        pl.BlockSpec((None, n_kv, per, HEAD_DIM), lambda b, g, pt: (b, 0, 0, 0)),
        pl.BlockSpec((None, n_new, width), lambda b, g, pt: (b, 0, cols["k"] // width)),
        pl.BlockSpec((None, n_new, width), lambda b, g, pt: (b, 0, cols["v"] // width)),
        pl.BlockSpec((None, n_new, LANE), lambda b, g, pt: (b, 0, 0)),
        pl.BlockSpec(memory_space=pl.ANY),
        pl.BlockSpec(memory_space=pl.ANY),
        pl.BlockSpec(memory_space=pl.ANY),
    ]
    return pl.pallas_call(
        functools.partial(_dec_c_kernel, li=li, page=page, n_pages=n_pages, n_new=n_new, n_heads=n_heads),
        grid_spec=pltpu.PrefetchScalarGridSpec(
            num_scalar_prefetch=1,
            grid=(nb, ng),
            in_specs=in_specs,
            out_specs=pl.BlockSpec((None, n_kv, per, HEAD_DIM), lambda b, g, pt: (b, 0, 0, 0)),
            scratch_shapes=[page_buf, page_buf, pltpu.VMEM((2, pg * n_heads, page), F32),
                            pltpu.SemaphoreType.DMA((3, 2)),
                            pltpu.VMEM((rows, 1), F32), pltpu.VMEM((rows, 1), F32),
                            pltpu.VMEM((rows, HEAD_DIM), F32),
                            pltpu.VMEM((n_heads, 1), F32), pltpu.VMEM((rows, 1), F32)],
        ),
        out_shape=jax.ShapeDtypeStruct((nb, n_kv, per, HEAD_DIM), F32),
        compiler_params=_params("arbitrary", "arbitrary"),
        name="attn_c_decode",
    )(page_table.reshape(-1), q4, proj3, proj3, logf3, cache_k, cache_v, cache_f_t)


def _pad_cols(w, n):
    return jnp.pad(w, ((0, 0), (0, n - w.shape[1])))


def _rope_tables(pos, reps):
    half = B_ROPE // 2
    inv = ROPE_THETA ** (-jnp.arange(half, dtype=F32) / half)
    ang = pos.astype(F32)[:, None] * inv[None, :]
    cos = jnp.tile(jnp.cos(ang), (reps, LANE // half))
    sin = jnp.tile(jnp.sin(ang), (reps, LANE // half))
    return cos, sin


def _decode_queries(q):
    b, t, n_kv, group, n_maps, w = q.shape
    qp = jnp.transpose(q, (0, 2, 4, 3, 1, 5))
    out = jnp.zeros((b, n_kv, n_maps, group, t, n_maps, w), q.dtype)
    for c in range(n_maps):
        out = out.at[:, :, c, :, :, c, :].set(qp[:, :, c])
    return out.reshape(b, n_kv, n_maps * group * t, n_maps * w)


def kernel(x_prompt, x_sample, cache_a_k, cache_a_v, cache_b_latent, cache_b_rope, cache_c_k, cache_c_v, cache_c_logf, page_table, rel_bias, w_in_even, a_lambda, a_subln_g, b_q_norm_g, b_w_uq, b_kv_norm_g, b_w_uk, b_w_uv, w_o_even, w_in_odd, c_forget_b, w_o_odd, g_mix_pre, g_mix_post, g_ffn_pre, g_ffn_post, w_ffn_up, w_ffn_down):
    batch, seq, d_model = x_prompt.shape
    dec_batch, dec_seq, _ = x_sample.shape
    depth = g_mix_pre.shape[0]
    n_even = w_in_even.shape[0]
    n_pool, page = cache_a_k.shape[1], cache_a_k.shape[2]
    n_pages = page_table.shape[1]
    past = n_pages * page
    a_heads = rel_bias.shape[1]
    a_group = a_heads // A_KV_HEADS
    a_half = HEAD_DIM // 2
    b_heads = b_w_uq.shape[2]
    q_rank = b_w_uq.shape[1]
    kv_rank = b_w_uk.shape[1]
    b_nope = b_w_uk.shape[3]
    c_heads = c_forget_b.shape[1]
    c_group = c_heads // C_KV_HEADS
    b_scale = (b_nope + B_ROPE) ** -0.5

    aq_cols = a_heads * HEAD_DIM
    ak_cols = A_KV_HEADS * HEAD_DIM
    ecols = {"aq": 0, "ak": aq_cols, "av": aq_cols + ak_cols, "bq": aq_cols + 2 * ak_cols}
    ecols["ckv"] = ecols["bq"] + q_rank
    ecols["kpe"] = ecols["ckv"] + kv_rank
    even_in = ecols["kpe"] + B_ROPE
    even_pad = -(-even_in // 512) * 512
    cq_cols = c_heads * HEAD_DIM
    ckv_cols = C_KV_HEADS * HEAD_DIM
    ocols = {"q": 0, "k": cq_cols, "v": cq_cols + ckv_cols, "f": cq_cols + 2 * ckv_cols}
    odd_in = ocols["f"] + c_heads
    odd_pad = -(-(ocols["f"] + LANE) // 512) * 512

    w_in_e = [_pad_cols(w_in_even[i], even_pad).astype(BF16) for i in range(n_even)]
    w_in_o = [_pad_cols(w_in_odd[i], odd_pad).astype(BF16) for i in range(w_in_odd.shape[0])]
    w_uq = [jnp.concatenate([b_w_uq[i][:, :, :b_nope].reshape(q_rank, -1),
                             b_w_uq[i][:, :, b_nope:].reshape(q_rank, -1)], axis=1).astype(BF16)
            for i in range(n_even)]
    w_ukT = [jnp.transpose(b_w_uk[i], (1, 2, 0)).astype(BF16) for i in range(n_even)]
    w_uv = [jnp.transpose(b_w_uv[i], (1, 0, 2)).astype(BF16) for i in range(n_even)]
    w_o_e = w_o_even.astype(BF16)
    w_o_o = w_o_odd.astype(BF16)
    w_up = w_ffn_up.astype(BF16)
    w_down = w_ffn_down.astype(BF16)

    lam_terms = lambda_terms(a_lambda)

    blk = min(ATT_BLOCK, seq)
    n_far = 2
    assert (n_far - 1) * blk + 1 > MAX_DISTANCE or seq // blk <= n_far
    ii = jnp.arange(blk)[:, None]
    jj = jnp.arange(blk)[None, :]
    tiles = jnp.concatenate([t5_bucket(d * blk + ii - jj) for d in range(n_far + 1)], axis=0)
    bias_p = bias_lookup(rel_bias, tiles).reshape(a_heads, n_far + 1, blk, blk)
    q_pos_s = past + jnp.arange(dec_seq)
    k_pos_s = jnp.arange(past + LANE)
    rel_s = jnp.pad(q_pos_s[:, None] - k_pos_s[None, :], ((0, 8 - dec_seq), (0, 0)))
    bias_s = bias_lookup(rel_bias, t5_bucket(rel_s))[:, :dec_seq]
    bias_s = bias_s.reshape(A_KV_HEADS, 1, a_group, dec_seq, -1)
    bias_s = jnp.broadcast_to(bias_s, (A_KV_HEADS, 2, a_group, dec_seq, bias_s.shape[-1]))
    bias_s = bias_s.reshape(A_KV_HEADS * 2 * a_group * dec_seq, -1)

    cos_p, sin_p = _rope_tables(jnp.arange(seq), batch)
    cos_s, sin_s = _rope_tables(q_pos_s, dec_batch)

    ca_k = cache_a_k.reshape(n_even, n_pool, page * A_KV_HEADS, HEAD_DIM)
    ca_v = cache_a_v.reshape(n_even, n_pool, page * A_KV_HEADS, HEAD_DIM)
    cc_k = cache_c_k.reshape(cache_c_k.shape[0], n_pool, page * C_KV_HEADS, HEAD_DIM)
    cc_v = cache_c_v.reshape(cache_c_v.shape[0], n_pool, page * C_KV_HEADS, HEAD_DIM)
    cb_rope_t = jnp.swapaxes(cache_b_rope, 2, 3)
    cc_logf_t = jnp.swapaxes(cache_c_logf, 2, 3)

    def trunk(x, sample):
        nb, t = (dec_batch, dec_seq) if sample else (batch, seq)
        m = nb * t
        x = x.reshape(m, d_model)
        cos, sin = (cos_s, sin_s) if sample else (cos_p, sin_p)
        rows_even, rows_odd = [], []
        for layer in range(depth):
            li = layer // 2
            if layer % 2 == 0:
                lam_init = 0.8 - 0.6 * math.exp(-0.3 * layer)
                lam = lam_terms[li, :1].reshape(1, 1)
                proj = norm_matmul(x, g_mix_pre[layer], w_in_e[li])
                k_a = proj[:, ecols["ak"]:ecols["av"]]
                v_a = proj[:, ecols["av"]:ecols["bq"]]
                qlat, qpe, c_kv, ckv_bf, kpe2, kpe_bf = mla_prep(
                    proj, b_q_norm_g[li], b_kv_norm_g[li], w_uq[li], w_ukT[li], cos, sin, ecols)
                k_pe = kpe2[:, :B_ROPE]
                if sample:
                    q_a = proj[:, :aq_cols].reshape(nb, t, A_KV_HEADS, a_group, 2, a_half)
                    proj3 = proj.reshape(nb, t, -1)
                    oa = attn_a_decode(page_table, lam, _decode_queries(q_a), proj3, ecols, bias_s, a_subln_g[li],
                                       ca_k, ca_v, li, lam_init)
                    a_out = jnp.transpose(oa.reshape(nb, A_KV_HEADS, a_group, t, HEAD_DIM),
                                          (0, 3, 1, 2, 4)).reshape(m, -1).astype(BF16)
                    qlat3 = jnp.transpose(qlat.reshape(nb, t, b_heads, kv_rank).astype(F32),
                                          (0, 2, 1, 3)).reshape(nb, b_heads * t, kv_rank)
                    qpe3 = jnp.transpose(qpe.reshape(nb, t, b_heads, B_ROPE).astype(F32),
                                         (0, 2, 1, 3)).reshape(nb, b_heads * t, B_ROPE)
                    ob = attn_b_decode(page_table, qlat3, qpe3, c_kv.reshape(nb, t, -1),
                                       kpe2.reshape(nb, t, -1), cache_b_latent, cb_rope_t, li, b_scale)
                    lat = jnp.transpose(ob.reshape(nb, b_heads, t, kv_rank), (0, 2, 1, 3)).reshape(m, -1)
                    b_out = head_matmul(lat, w_uv[li])
                else:
                    kv_bf = proj[:, ecols["ak"]:ecols["bq"]].astype(BF16)
                    a_out = attn_a_prompt(lam, proj, kv_bf, bias_p, a_subln_g[li], nb, a_heads,
                                          lam_init, ecols["aq"])
                    b_out = attn_b_prompt(qlat, qpe, ckv_bf, kpe_bf, w_uv[li], nb, b_scale)
                h = jnp.concatenate([a_out, b_out], axis=1)
                x = matmul_norm_residual(h, w_o_e[li], g_mix_post[layer], x)
                rows_even.append((k_a.reshape(nb, t, A_KV_HEADS, HEAD_DIM),
                                  v_a.reshape(nb, t, A_KV_HEADS, HEAD_DIM),
                                  c_kv.reshape(nb, t, kv_rank), k_pe.reshape(nb, t, B_ROPE)))
            else:
                proj = norm_matmul(x, g_mix_pre[layer], w_in_o[li])
                k_c = proj[:, ocols["k"]:ocols["v"]]
                v_c = proj[:, ocols["v"]:ocols["f"]]
                logf_pad, c_pad = logf_cumsum(proj, ocols["f"], c_forget_b[li], nb, cumulate=not sample)
                logf = logf_pad[:, :c_heads]
                if sample:
                    q_c = jnp.transpose(proj[:, :cq_cols].reshape(nb, t, C_KV_HEADS, c_group, HEAD_DIM),
                                        (0, 2, 1, 3, 4)).reshape(nb, C_KV_HEADS, t * c_group, HEAD_DIM)
                    oc = attn_c_decode(page_table, q_c, proj.reshape(nb, t, -1),
                                       logf_pad.reshape(nb, t, LANE), ocols, cc_k, cc_v, cc_logf_t, li)
                    h = jnp.transpose(oc.reshape(nb, C_KV_HEADS, t, c_group, HEAD_DIM),
                                      (0, 2, 1, 3, 4)).reshape(m, -1).astype(BF16)
                else:
                    kv_bf = proj[:, ocols["k"]:ocols["f"]].astype(BF16)
                    c_t = jnp.transpose(c_pad[:, :c_heads].reshape(nb, t, c_heads), (0, 2, 1))
                    h = attn_c_prompt(proj, kv_bf, c_t, nb, c_heads)
                x = matmul_norm_residual(h, w_o_o[li], g_mix_post[layer], x)
                rows_odd.append((k_c.reshape(nb, t, C_KV_HEADS, HEAD_DIM),
                                 v_c.reshape(nb, t, C_KV_HEADS, HEAD_DIM),
                                 logf.reshape(nb, t, c_heads)))
            u = norm_matmul(x, g_ffn_pre[layer], w_up[layer], act=True, out_dtype=BF16)
            x = matmul_norm_residual(u, w_down[layer], g_ffn_post[layer], x)
        even = [jnp.stack(r) for r in zip(*rows_even)]
        odd = [jnp.stack(r) for r in zip(*rows_odd)]
        return x.reshape(nb, t, d_model), even, odd

    y_prompt, (pa_k, pa_v, pb_lat, pb_rope), (pc_k, pc_v, pc_logf) = trunk(x_prompt, False)
    y_sample, (sa_k, sa_v, sb_lat, sb_rope), (sc_k, sc_v, sc_logf) = trunk(x_sample, True)
    return (y_prompt, y_sample, pa_k, sa_k, pa_v, sa_v, pb_lat, sb_lat, pb_rope, sb_rope,
            pc_k, sc_k, pc_v, sc_v, pc_logf, sc_logf)
```
